```python
import math
import jax, jax.numpy as jnp
from jax import lax
import numpy as np


D_MODEL = 1024
BATCH = 32
SEQ = 2048
DEPTH = 1
DEC_BATCH = 16
DEC_SEQ = 2048
PAST_LEN = 128

N_MEM = 256
EPS = 1e-6
S5_WIDTH = D_MODEL // 2
S5_GROUP = 16
S5_GROUPS = S5_WIDTH // S5_GROUP
S5_STATE = 64
DIFF_HEADS = 4
DIFF_DH = D_MODEL // 16
DIFF_VDH = 2 * DIFF_DH
DIFF_QK_WIDTH = 2 * DIFF_HEADS * DIFF_DH
DIFF_WIDTH = DIFF_HEADS * DIFF_VDH
X_HEADS = 4
X_DH = D_MODEL // 8
X_WIDTH = X_HEADS * X_DH
BRANCH_WIDTH = D_MODEL // 2
N_BRANCH = 3
D_FF = -(-8 * D_MODEL // (3 * 256)) * 256
ROPE_THETA = 500000.0
ROPE_DIM = DIFF_DH // 4
Q_BLOCK = 128
IN_SPLITS = (S5_WIDTH,
             S5_WIDTH + DIFF_QK_WIDTH,
             S5_WIDTH + 2 * DIFF_QK_WIDTH,
             S5_WIDTH + 2 * DIFF_QK_WIDTH + DIFF_WIDTH,
             S5_WIDTH + 2 * DIFF_QK_WIDTH + DIFF_WIDTH + X_WIDTH)
IN_COLS = IN_SPLITS[-1] + N_BRANCH * D_MODEL

kernel_name = 'hybrid_s5_diffattn_memory_encoder'


def _rmsnorm(x, g):
    xf = x.astype(jnp.float32)
    y = xf * lax.rsqrt(jnp.mean(xf * xf, axis=-1, keepdims=True) + EPS)
    return (y * g.astype(jnp.float32)).astype(x.dtype)


def _rope_tables(L):
    inv = 1.0 / (ROPE_THETA ** (jnp.arange(0, ROPE_DIM, 2, dtype=jnp.float32) / ROPE_DIM))
    ang = jnp.arange(L, dtype=jnp.float32)[:, None] * inv[None, :]
    return jnp.cos(ang), jnp.sin(ang)


def _partial_rope(x, cos, sin):
    half = ROPE_DIM // 2
    xr = x[..., :ROPE_DIM].astype(jnp.float32)
    x1, x2 = xr[..., :half], xr[..., half:]
    c = cos[None, :, None, :]
    s = sin[None, :, None, :]
    rot = jnp.concatenate([x1 * c - x2 * s, x2 * c + x1 * s], axis=-1).astype(x.dtype)
    return jnp.concatenate([rot, x[..., ROPE_DIM:]], axis=-1)


def _ssm_combine(e1, e2):
    a1, b1 = e1
    a2, b2 = e2
    return a1 * a2, a2 * b1 + b2


def _s5_scan_dir(u, lam, step, b, c):
    lam_bar = jnp.exp(lam * step[:, None])
    b_bar = ((lam_bar - 1.0) / lam)[..., None] * b
    bu = jnp.einsum('lgh,gph->lgp', u.astype(jnp.complex64), b_bar)
    a = jnp.broadcast_to(lam_bar, bu.shape)
    _, xs = lax.associative_scan(_ssm_combine, (a, bu), axis=0)
    return jnp.real(jnp.einsum('lgp,ghp->lgh', xs, c))


def _s5_branch(u, lam_re, lam_im, log_step, b_re, b_im, c_re, c_im, d, w_glu, b_glu):
    dtype = u.dtype
    Bn, L, _ = u.shape
    f32 = jnp.float32
    lam = lax.complex(lam_re.astype(f32), lam_im.astype(f32))
    step = jnp.exp(log_step.astype(f32))
    bmat = lax.complex(b_re.astype(f32), b_im.astype(f32))
    cmat = lax.complex(c_re.astype(f32), c_im.astype(f32))
    uf = u.astype(f32)
    ug = uf.reshape(Bn, L, S5_GROUPS, S5_GROUP)

    def one_seq(us):
        y_f = _s5_scan_dir(us, lam[0], step[0], bmat[0], cmat[0])
        y_b = jnp.flip(_s5_scan_dir(jnp.flip(us, 0), lam[1], step[1], bmat[1], cmat[1]), 0)
        return y_f + y_b

    y = lax.map(one_seq, ug).reshape(Bn, L, S5_WIDTH) + d.astype(f32) * uf
    z = jax.nn.gelu(y).astype(dtype)
    return z * jax.nn.sigmoid(z @ w_glu + b_glu)


def _diff_attention(q, k, v, lam, subln_g, lambda_init):
    Bn, L = q.shape[0], q.shape[1]
    nb = L // Q_BLOCK
    qb = q.reshape(Bn, nb, Q_BLOCK, 2 * DIFF_HEADS, DIFF_DH).transpose(1, 0, 2, 3, 4)
    scale = DIFF_DH ** -0.5

    def block(qblk):
        s = jnp.einsum('bqhd,bkhd->bhqk', qblk, k).astype(jnp.float32) * scale
        probs = jax.nn.softmax(s, axis=-1).reshape(Bn, DIFF_HEADS, 2, Q_BLOCK, L)
        w = probs[:, :, 0] - lam * probs[:, :, 1]
        return jnp.einsum('bhqk,bkhe->bqhe', w.astype(v.dtype), v)

    o = lax.map(block, qb)
    o = o.transpose(1, 0, 2, 3, 4).reshape(Bn, L, DIFF_HEADS, DIFF_VDH)
    o = _rmsnorm(o, subln_g) * (1.0 - lambda_init)
    return o.reshape(Bn, L, DIFF_WIDTH)


def _cross_attention(q, mk, mv):
    Bn, L = q.shape[0], q.shape[1]
    s = jnp.einsum('blhd,bmhd->bhlm', q, mk).astype(jnp.float32) * (X_DH ** -0.5)
    probs = jax.nn.softmax(s, axis=-1).astype(mv.dtype)
    return jnp.einsum('bhlm,bmhd->blhd', probs, mv).reshape(Bn, L, X_WIDTH)


def _layer(x, mem, l, p):
    Bn, L, _ = x.shape
    xn = _rmsnorm(x, p['norm_mix'][l])
    mn = _rmsnorm(mem, p['norm_mem'][l])
    h = xn @ p['w_in'][l]
    u, q, k, v, xq, g = jnp.split(h, IN_SPLITS, axis=-1)
    y_s5 = _s5_branch(u, p['s5_lambda_re'][l], p['s5_lambda_im'][l], p['s5_log_step'][l],
                      p['s5_b_re'][l], p['s5_b_im'][l], p['s5_c_re'][l], p['s5_c_im'][l],
                      p['s5_d'][l], p['s5_w_glu'][l], p['s5_b_glu'][l])
    cos, sin = _rope_tables(L)
    q = _partial_rope(q.reshape(Bn, L, 2 * DIFF_HEADS, DIFF_DH), cos, sin)
    k = _partial_rope(k.reshape(Bn, L, 2 * DIFF_HEADS, DIFF_DH), cos, sin)
    v = v.reshape(Bn, L, DIFF_HEADS, DIFF_VDH)
    lambda_init = 0.8 - 0.6 * math.exp(-0.3 * l)
    f32 = jnp.float32
    lam = (jnp.exp(jnp.sum(p['diff_lambda_q1'][l].astype(f32) * p['diff_lambda_k1'][l].astype(f32)))
           - jnp.exp(jnp.sum(p['diff_lambda_q2'][l].astype(f32) * p['diff_lambda_k2'][l].astype(f32)))
           + lambda_init)
    y_diff = _diff_attention(q, k, v, lam, p['diff_subln'][l], lambda_init)
    mk, mv = jnp.split(mn @ p['w_mem_kv'][l], 2, axis=-1)
    mk = mk.reshape(Bn, N_MEM, X_HEADS, X_DH)
    mv = mv.reshape(Bn, N_MEM, X_HEADS, X_DH)
    y_x = _cross_attention(xq.reshape(Bn, L, X_HEADS, X_DH), mk, mv)
    branches = (y_s5, y_diff, y_x)
    merged = None
    for n in range(N_BRANCH):
        term = jax.nn.sigmoid(g[..., n * D_MODEL:(n + 1) * D_MODEL]) * (branches[n] @ p['w_up'][l, n])
        merged = term if merged is None else merged + term
    x = x + merged @ p['w_out'][l]
    xn2 = _rmsnorm(x, p['norm_ffn'][l])
    gate, up = jnp.split(xn2 @ p['w_ffn_in'][l], 2, axis=-1)
    return x + (jax.nn.silu(gate) * up) @ p['w_ffn_out'][l]


def _trunk(x, mem, p):
    for l in range(DEPTH):
        x = _layer(x, mem, l, p)
    return _rmsnorm(x, p['norm_final'])


def setup_inputs(seed: int = 0) -> dict:
    key = jax.random.key(seed)
    ks = jax.random.split(key, 32)
    f32 = jnp.float32

    def nrm(k, shape, scale):
        return jax.random.normal(k, shape, f32) * scale

    n_idx = jnp.arange(S5_STATE, dtype=f32)
    return {
        'x_prompt': nrm(ks[0], (BATCH, SEQ, D_MODEL), 1.0),
        'x_sample': nrm(ks[1], (DEC_BATCH, DEC_SEQ, D_MODEL), 1.0),
        'mem_prompt': nrm(ks[2], (BATCH, N_MEM, D_MODEL), 1.0),
        'mem_sample': nrm(ks[3], (DEC_BATCH, N_MEM, D_MODEL), 1.0),
        'norm_mix': 1.0 + nrm(ks[4], (DEPTH, D_MODEL), 0.02),
        'norm_mem': 1.0 + nrm(ks[5], (DEPTH, D_MODEL), 0.02),
        'w_in': nrm(ks[6], (DEPTH, D_MODEL, IN_COLS), D_MODEL ** -0.5),
        's5_lambda_re': -0.5 + nrm(ks[7], (DEPTH, 2, S5_GROUPS, S5_STATE), 0.01),
        's5_lambda_im': math.pi * n_idx + nrm(ks[8], (DEPTH, 2, S5_GROUPS, S5_STATE), 0.01),
        's5_log_step': jax.random.uniform(ks[9], (DEPTH, 2, S5_GROUPS), f32,
                                          minval=math.log(1e-3), maxval=math.log(1e-1)),
        's5_b_re': nrm(ks[10], (DEPTH, 2, S5_GROUPS, S5_STATE, S5_GROUP), (2 * S5_GROUP) ** -0.5),
        's5_b_im': nrm(ks[11], (DEPTH, 2, S5_GROUPS, S5_STATE, S5_GROUP), (2 * S5_GROUP) ** -0.5),
        's5_c_re': nrm(ks[12], (DEPTH, 2, S5_GROUPS, S5_GROUP, S5_STATE), (2 * S5_STATE) ** -0.5),
        's5_c_im': nrm(ks[13], (DEPTH, 2, S5_GROUPS, S5_GROUP, S5_STATE), (2 * S5_STATE) ** -0.5),
        's5_d': nrm(ks[14], (DEPTH, S5_WIDTH), 1.0),
        's5_w_glu': nrm(ks[15], (DEPTH, S5_WIDTH, S5_WIDTH), S5_WIDTH ** -0.5),
        's5_b_glu': nrm(ks[16], (DEPTH, S5_WIDTH), 0.01),
        'diff_lambda_q1': nrm(ks[17], (DEPTH, DIFF_DH), 0.1),
        'diff_lambda_k1': nrm(ks[18], (DEPTH, DIFF_DH), 0.1),
        'diff_lambda_q2': nrm(ks[19], (DEPTH, DIFF_DH), 0.1),
        'diff_lambda_k2': nrm(ks[20], (DEPTH, DIFF_DH), 0.1),
        'diff_subln': 1.0 + nrm(ks[21], (DEPTH, DIFF_VDH), 0.02),
        'w_mem_kv': nrm(ks[22], (DEPTH, D_MODEL, 2 * X_WIDTH), D_MODEL ** -0.5),
        'w_up': nrm(ks[23], (DEPTH, N_BRANCH, BRANCH_WIDTH, D_MODEL), BRANCH_WIDTH ** -0.5),
        'w_out': nrm(ks[24], (DEPTH, D_MODEL, D_MODEL), D_MODEL ** -0.5),
        'norm_ffn': 1.0 + nrm(ks[25], (DEPTH, D_MODEL), 0.02),
        'w_ffn_in': nrm(ks[26], (DEPTH, D_MODEL, 2 * D_FF), D_MODEL ** -0.5),
        'w_ffn_out': nrm(ks[27], (DEPTH, D_FF, D_MODEL), D_FF ** -0.5),
        'norm_final': 1.0 + nrm(ks[28], (D_MODEL,), 0.02),
    }


def reference(x_prompt, x_sample, mem_prompt, mem_sample, norm_mix, norm_mem, w_in,
              s5_lambda_re, s5_lambda_im, s5_log_step, s5_b_re, s5_b_im, s5_c_re, s5_c_im,
              s5_d, s5_w_glu, s5_b_glu, diff_lambda_q1, diff_lambda_k1, diff_lambda_q2,
              diff_lambda_k2, diff_subln, w_mem_kv, w_up, w_out, norm_ffn, w_ffn_in,
              w_ffn_out, norm_final):
    p = dict(norm_mix=norm_mix, norm_mem=norm_mem, w_in=w_in,
             s5_lambda_re=s5_lambda_re, s5_lambda_im=s5_lambda_im, s5_log_step=s5_log_step,
             s5_b_re=s5_b_re, s5_b_im=s5_b_im, s5_c_re=s5_c_re, s5_c_im=s5_c_im,
             s5_d=s5_d, s5_w_glu=s5_w_glu, s5_b_glu=s5_b_glu,
             diff_lambda_q1=diff_lambda_q1, diff_lambda_k1=diff_lambda_k1,
             diff_lambda_q2=diff_lambda_q2, diff_lambda_k2=diff_lambda_k2,
             diff_subln=diff_subln, w_mem_kv=w_mem_kv, w_up=w_up, w_out=w_out,
             norm_ffn=norm_ffn, w_ffn_in=w_ffn_in, w_ffn_out=w_ffn_out,
             norm_final=norm_final)
    y_prompt = _trunk(x_prompt, mem_prompt, p)
    y_sample = _trunk(x_sample, mem_sample, p)
    return (y_prompt, y_sample)
```

```python
import functools
import math

import jax
import jax.numpy as jnp
from jax import lax
from jax.experimental import pallas as pl
from jax.experimental.pallas import tpu as pltpu

F32 = jnp.float32
BF16 = jnp.bfloat16

D_MODEL = 1024
N_MEM = 256
EPS = 1e-6
S5_WIDTH = 512
S5_GROUP = 16
S5_GROUPS = 32
S5_STATE = 64
DIFF_HEADS = 4
DIFF_DH = 64
DIFF_VDH = 128
X_HEADS = 4
X_DH = 128
BRANCH_WIDTH = 512
D_FF = 2816
ROPE_THETA = 500000.0
ROPE_DIM = 16
LAMBDA_INIT = 0.8 - 0.6 * math.exp(-0.3 * 0)
A_COLS = 5 * 512
S5_ROW = 16
S5_LANES = S5_ROW * S5_GROUP

VMEM_LIMIT = 56 * 1024 * 1024


def _const_spec(shape):
    nd = len(shape)
    return pl.BlockSpec(shape, lambda *_: (0,) * nd, pipeline_mode=pl.Buffered(1))


def _params(sem):
    return pltpu.CompilerParams(dimension_semantics=sem, vmem_limit_bytes=VMEM_LIMIT)


def _rms(x, g):
    return x * lax.rsqrt(jnp.mean(x * x, axis=-1, keepdims=True) + EPS) * g


def _inproj_kernel(x_ref, g_ref, w_ref, cos_ref, sa_ref, sb_ref,
                   u_ref, q_ref, k_ref, v_ref, xq_ref):
    xn = _rms(x_ref[...], g_ref[...]).astype(BF16)
    h = jnp.dot(xn, w_ref[...], preferred_element_type=F32)
    u_ref[...] = h[:, 0:512]
    cos = cos_ref[...]
    sa = sa_ref[...]
    sb = sb_ref[...]
    for j in range(4):
        lo = 128 * j
        tq = h[:, 512 + lo:512 + lo + 128]
        tk = h[:, 1024 + lo:1024 + lo + 128]
        rq = tq * cos + pltpu.roll(tq, 8, 1) * sa + pltpu.roll(tq, 120, 1) * sb
        rk = tk * cos + pltpu.roll(tk, 8, 1) * sa + pltpu.roll(tk, 120, 1) * sb
        q_ref[:, lo:lo + 128] = (rq * (DIFF_DH ** -0.5)).astype(BF16)
        k_ref[:, lo:lo + 128] = rk.astype(BF16)
    v_ref[...] = h[:, 1536:2048].astype(BF16)
    xq_ref[...] = h[:, 2048:2560].astype(BF16)


def _inproj(x2, norm_mix, w_a, cos_t, sa_t, sb_t, seq_len, tm=512):
    t = x2.shape[0]
    nper = seq_len // tm
    row = lambda i: (i, 0)
    tab = pl.BlockSpec((tm, 128), lambda i: (i % nper, 0))
    outs = [jax.ShapeDtypeStruct((t, 512), F32)] + [jax.ShapeDtypeStruct((t, 512), BF16)] * 4
    return pl.pallas_call(
        _inproj_kernel,
        out_shape=outs,
        grid=(t // tm,),
        in_specs=[pl.BlockSpec((tm, D_MODEL), row), _const_spec((1, D_MODEL)),
                  _const_spec((D_MODEL, A_COLS)), tab, tab, tab],
        out_specs=[pl.BlockSpec((tm, 512), row)] * 5,
        compiler_params=_params(("parallel",)),
        name="inproj",
    )(x2, norm_mix, w_a, cos_t, sa_t, sb_t)


def _memkv_kernel(m_ref, g_ref, w_ref, mk_ref, mv_ref):
    mn = _rms(m_ref[...], g_ref[...]).astype(BF16)
    h = jnp.dot(mn, w_ref[...], preferred_element_type=F32)
    mk_ref[...] = h[:, 0:512].astype(BF16)
    mv_ref[...] = h[:, 512:1024].astype(BF16)


def _memkv(mem2, norm_mem, w_kv, tm=512):
    t = mem2.shape[0]
    row = lambda i: (i, 0)
    return pl.pallas_call(
        _memkv_kernel,
        out_shape=[jax.ShapeDtypeStruct((t, 512), BF16)] * 2,
        grid=(t // tm,),
        in_specs=[pl.BlockSpec((tm, D_MODEL), row), _const_spec((1, D_MODEL)),
                  _const_spec((D_MODEL, 1024))],
        out_specs=[pl.BlockSpec((tm, 512), row)] * 2,
        compiler_params=_params(("parallel",)),
        name="memkv",
    )(mem2, norm_mem, w_kv)


def _cexp(mag_arg, ang):
    m = jnp.exp(mag_arg)
    return m * jnp.cos(ang), m * jnp.sin(ang)


def _s5_param_kernel(lr_row, li_row, lr_col, li_col, ls_ref, bt_r, bt_i, ct_r, ct_i,
                     m1r_ref, m1i_ref, wr_ref, wi_ref, g2_ref, dr_ref, di_ref):
    s_idx = (lax.broadcasted_iota(jnp.int32, (S5_LANES, 1), 0) // S5_GROUP).astype(F32)
    i_idx = (lax.broadcasted_iota(jnp.int32, (1, S5_LANES), 1) // S5_GROUP).astype(F32)
    for d in range(2):
        step = jnp.exp(ls_ref[d])
        lr = lr_row[d]
        li = li_row[d]
        ar = lr * step
        ai = li * step
        lb_r, lb_i = _cexp(ar, ai)
        den = lr * lr + li * li
        nr = lb_r - 1.0
        cf_r = (nr * lr + lb_i * li) / den
        cf_i = (lb_i * lr - nr * li) / den
        btr = bt_r[d]
        bti = bt_i[d]
        bb_r = cf_r * btr - cf_i * bti
        bb_i = cf_r * bti + cf_i * btr
        e = (15.0 - s_idx) if d == 0 else s_idx
        pw_r, pw_i = _cexp(e * ar, e * ai)
        m1r_ref[d] = pw_r * bb_r - pw_i * bb_i
        m1i_ref[d] = pw_r * bb_i + pw_i * bb_r
        dr, di = _cexp(16.0 * ar, 16.0 * ai)
        dr_ref[d] = dr
        di_ref[d] = di

        acr = lr_col[d] * step
        aci = li_col[d] * step
        ctr = ct_r[d]
        cti = ct_i[d]
        jw = (i_idx + 1.0) if d == 0 else (16.0 - i_idx)
        qr, qi = _cexp(jw * acr, jw * aci)
        wr_ref[d] = ctr * qr - cti * qi
        wi_ref[d] = -(ctr * qi + cti * qr)
        gr, gi = _cexp(i_idx * acr, i_idx * aci)
        cl_r = ctr * gr - cti * gi
        cl_i = ctr * gi + cti * gr
        g2_ref[d] = (jnp.dot(bb_r[0:16], cl_r, preferred_element_type=F32,
                             precision=lax.Precision.HIGHEST)
                     - jnp.dot(bb_i[0:16], cl_i, preferred_element_type=F32,
                               precision=lax.Precision.HIGHEST))


def _s5_matrices(lam_re, lam_im, log_step, b_re, b_im, c_re, c_im):
    G = S5_GROUPS
    lr_row = lam_re.reshape(2, G, 1, S5_STATE)
    li_row = lam_im.reshape(2, G, 1, S5_STATE)
    lr_col = lam_re.reshape(2, G, S5_STATE, 1)
    li_col = lam_im.reshape(2, G, S5_STATE, 1)
    ls = log_step.reshape(2, G, 1, 1)
    bt = lambda b: jnp.tile(jnp.swapaxes(b, -1, -2), (1, 1, S5_ROW, 1))
    ct = lambda c: jnp.tile(jnp.swapaxes(c, -1, -2), (1, 1, 1, S5_ROW))
    blk = lambda a, b: pl.BlockSpec((2, None, a, b), lambda g: (0, g, 0, 0))
    sd = lambda a, b: jax.ShapeDtypeStruct((2, G, a, b), F32)
    m1r, m1i, wr, wi, g2, dr, di = pl.pallas_call(
        _s5_param_kernel,
        out_shape=[sd(256, 64), sd(256, 64), sd(64, 256), sd(64, 256), sd(16, 256),
                   sd(1, 64), sd(1, 64)],
        grid=(G,),
        in_specs=[blk(1, 64), blk(1, 64), blk(64, 1), blk(64, 1), blk(1, 1),
                  blk(256, 64), blk(256, 64), blk(64, 256), blk(64, 256)],
        out_specs=[blk(256, 64), blk(256, 64), blk(64, 256), blk(64, 256), blk(16, 256),
                   blk(1, 64), blk(1, 64)],
        compiler_params=_params(("parallel",)),
        name="s5_params",
    )(lr_row, li_row, lr_col, li_col, ls, bt(b_re), bt(b_im), ct(c_re), ct(c_im))

    m1 = jnp.concatenate([m1r[0], m1r[1], m1i[0], m1i[1]], axis=-1)
    wa = jnp.concatenate([wr[0], wr[1]], axis=1)
    wb = jnp.concatenate([wi[0], wi[1]], axis=1)
    kf = g2[0].reshape(G, 16, 16, 16)
    kb = g2[1].reshape(G, 16, 16, 16)
    ribbon = jnp.concatenate([kb[:, :, :0:-1], kf[:, :, 0:1] + kb[:, :, 0:1], kf[:, :, 1:]],
                             axis=2).reshape(G, 16, 31 * 16)
    tz = jnp.stack([ribbon[:, :, 16 * (15 - s):16 * (15 - s) + 256] for s in range(16)],
                   axis=1).reshape(G, 256, 256)
    m2 = jnp.concatenate([tz, wa, wb], axis=1)
    dec_r = jnp.concatenate([dr[0], dr[1]], axis=-1)
    dec_i = jnp.concatenate([di[0], di[1]], axis=-1)
    return m1.astype(BF16), m2.astype(BF16), dec_r, dec_i


def _s5_kernel(u_ref, m1_ref, m2_ref, dr_ref, di_ref, y_ref, ea, eb, sa, sb, *, nb, nc):
    e = jnp.dot(u_ref[...], m1_ref[...], preferred_element_type=F32)
    ea[...] = e[:, 0:128]
    eb[...] = e[:, 128:256]
    ar = dr_ref[...]
    ai = di_ref[...]
    fwd = lax.broadcasted_iota(jnp.int32, (nb, 128), 1) < S5_STATE

    def scan(c, carry):
        s_re, s_im = carry
        r0 = pl.multiple_of(c * nb, nb)
        r1 = pl.multiple_of((nc - 1 - c) * nb, nb)
        sa[pl.ds(r0, nb), :] = s_re
        sb[pl.ds(r0, nb), :] = s_im
        in_re = jnp.where(fwd, ea[pl.ds(r0, nb), :], ea[pl.ds(r1, nb), :])
        in_im = jnp.where(fwd, eb[pl.ds(r0, nb), :], eb[pl.ds(r1, nb), :])
        return (ar * s_re - ai * s_im + in_re, ar * s_im + ai * s_re + in_im)

    zero = jnp.zeros((nb, 128), F32)
    lax.fori_loop(0, nc, scan, (zero, zero))

    def unflip(c, _):
        r0 = pl.multiple_of(c * nb, nb)
        r1 = pl.multiple_of((nc - 1 - c) * nb, nb)
        ea[pl.ds(r0, nb), :] = jnp.where(fwd, sa[pl.ds(r0, nb), :], sa[pl.ds(r1, nb), :])
        eb[pl.ds(r0, nb), :] = jnp.where(fwd, sb[pl.ds(r0, nb), :], sb[pl.ds(r1, nb), :])
        return 0

    lax.fori_loop(0, nc, unflip, 0)
    y = jnp.dot(u_ref[...], m2_ref[0:256, :], preferred_element_type=F32)
    y += jnp.dot(ea[...].astype(BF16), m2_ref[256:384, :], preferred_element_type=F32)
    y += jnp.dot(eb[...].astype(BF16), m2_ref[384:512, :], preferred_element_type=F32)
    y_ref[...] = y


def _s5_core(u_rows, m1, m2, dec_r, dec_i, nb, nc):
    m = nb * nc
    grp = lambda a, b: pl.BlockSpec((None, a, b), lambda g: (g, 0, 0))
    return pl.pallas_call(
        functools.partial(_s5_kernel, nb=nb, nc=nc),
        out_shape=jax.ShapeDtypeStruct((S5_GROUPS, m, S5_LANES), F32),
        grid=(S5_GROUPS,),
        in_specs=[grp(m, 256), grp(256, 256), grp(512, 256), grp(1, 128), grp(1, 128)],
        out_specs=grp(m, 256),
        scratch_shapes=[pltpu.VMEM((m, 128), F32)] * 4,
        compiler_params=_params(("parallel",)),
        name="s5_core",
    )(u_rows, m1, m2, dec_r, dec_i)


def _diff_kernel(q_ref, k_ref, v_ref, lq1, lk1, lq2, lk2, sg_ref, o_ref):
    lam = (jnp.exp(jnp.sum(lq1[...] * lk1[...], axis=-1, keepdims=True))
           - jnp.exp(jnp.sum(lq2[...] * lk2[...], axis=-1, keepdims=True))
           + LAMBDA_INIT)
    first = lax.broadcasted_iota(jnp.int32, (1, 128), 1) < DIFF_DH
    zero = jnp.zeros((), BF16)
    for n in range(DIFF_HEADS):
        lo = 128 * n
        q = q_ref[:, lo:lo + 128]
        k = k_ref[:, lo:lo + 128]
        es = []
        for c in range(2):
            qc = jnp.where(first if c == 0 else jnp.logical_not(first), q, zero)
            s = lax.dot_general(qc, k, (((1,), (1,)), ((), ())), preferred_element_type=F32)
            e = jnp.exp(s - jnp.max(s, axis=-1, keepdims=True))
            es.append((e, jnp.sum(e, axis=-1, keepdims=True)))
        (e1, l1), (e2, l2) = es
        w = e1 * (1.0 / l1) - e2 * (lam / l2)
        o = jnp.dot(w.astype(BF16), v_ref[:, lo:lo + 128], preferred_element_type=F32)
        o = _rms(o, sg_ref[...]) * (1.0 - LAMBDA_INIT)
        o_ref[:, lo:lo + 128] = o.astype(BF16)


def _diff_attention(q, k, v, lq1, lk1, lq2, lk2, subln, nbatch, seq_len, tq=256):
    nq = seq_len // tq
    vec = _const_spec((1, DIFF_DH))
    kv = pl.BlockSpec((seq_len, 512), lambda b, i: (b, 0))
    return pl.pallas_call(
        _diff_kernel,
        out_shape=jax.ShapeDtypeStruct(q.shape, BF16),
        grid=(nbatch, nq),
        in_specs=[pl.BlockSpec((tq, 512), lambda b, i: (b * nq + i, 0)), kv, kv,
                  vec, vec, vec, vec, _const_spec((1, DIFF_VDH))],
        out_specs=pl.BlockSpec((tq, 512), lambda b, i: (b * nq + i, 0)),
        compiler_params=_params(("parallel", "parallel")),
        name="diff_attention",
    )(q, k, v, lq1, lk1, lq2, lk2, subln)


def _merge_kernel(x_ref, u_ref, ys_ref, yd_ref, xq_ref, mk_ref, mv_ref,
                  gmix_ref, wg_ref, d_ref, wglu_ref, bglu_ref, wup_ref, wout_ref, o_ref):
    x = x_ref[...]
    xn = _rms(x, gmix_ref[...]).astype(BF16)

    y = ys_ref[...] + d_ref[...] * u_ref[...]
    z = jax.nn.gelu(y)
    zg = jnp.dot(z.astype(BF16), wglu_ref[...], preferred_element_type=F32) + bglu_ref[...]
    y_s5 = z * jax.nn.sigmoid(zg)

    heads = []
    for h in range(X_HEADS):
        lo = 128 * h
        s = lax.dot_general(xq_ref[:, lo:lo + 128], mk_ref[:, lo:lo + 128],
                            (((1,), (1,)), ((), ())), preferred_element_type=F32)
        s = s * (X_DH ** -0.5)
        e = jnp.exp(s - jnp.max(s, axis=-1, keepdims=True))
        p = e * (1.0 / jnp.sum(e, axis=-1, keepdims=True))
        heads.append(jnp.dot(p.astype(BF16), mv_ref[:, lo:lo + 128],
                             preferred_element_type=F32))
    y_x = jnp.concatenate(heads, axis=-1)

    branches = (y_s5.astype(BF16), yd_ref[...], y_x.astype(BF16))
    merged = None
    for n in range(3):
        g = jnp.dot(xn, wg_ref[:, n * D_MODEL:(n + 1) * D_MODEL], preferred_element_type=F32)
        up = jnp.dot(branches[n], wup_ref[n], preferred_element_type=F32)
        term = jax.nn.sigmoid(g) * up
        merged = term if merged is None else merged + term
    o_ref[...] = x + jnp.dot(merged.astype(BF16), wout_ref[...], preferred_element_type=F32)


def _merge(x2, u, ys, yd, xq, mk, mv, norm_mix, w_g, s5_d, w_glu, b_glu, w_up, w_out,
           seq_len, tm=256):
    t = x2.shape[0]
    nper = seq_len // tm
    row = lambda i: (i, 0)
    r512 = pl.BlockSpec((tm, 512), row)
    mem = pl.BlockSpec((N_MEM, 512), lambda i: (i // nper, 0))
    return pl.pallas_call(
        _merge_kernel,
        out_shape=jax.ShapeDtypeStruct((t, D_MODEL), F32),
        grid=(t // tm,),
        in_specs=[pl.BlockSpec((tm, D_MODEL), row), r512, r512, r512, r512, mem, mem,
                  _const_spec((1, D_MODEL)), _const_spec((D_MODEL, 3 * D_MODEL)),
                  _const_spec((1, 512)), _const_spec((512, 512)), _const_spec((1, 512)),
                  _const_spec((3, BRANCH_WIDTH, D_MODEL)), _const_spec((D_MODEL, D_MODEL))],
        out_specs=pl.BlockSpec((tm, D_MODEL), row),
        compiler_params=_params(("parallel",)),
        name="merge",
    )(x2, u, ys, yd, xq, mk, mv, norm_mix, w_g, s5_d, w_glu, b_glu, w_up, w_out)


def _ffn_kernel(x_ref, g_ref, win_ref, wout_ref, gf_ref, o_ref):
    x = x_ref[...]
    xn = _rms(x, g_ref[...]).astype(BF16)
    gate = jnp.dot(xn, win_ref[:, 0:D_FF], preferred_element_type=F32)
    up = jnp.dot(xn, win_ref[:, D_FF:2 * D_FF], preferred_element_type=F32)
    act = (jax.nn.silu(gate) * up).astype(BF16)
    x = x + jnp.dot(act, wout_ref[...], preferred_element_type=F32)
    o_ref[...] = _rms(x, gf_ref[...])


def _ffn(x1, norm_ffn, w_in, w_out, norm_final, tm=256):
    t = x1.shape[0]
    row = lambda i: (i, 0)
    return pl.pallas_call(
        _ffn_kernel,
        out_shape=jax.ShapeDtypeStruct((t, D_MODEL), F32),
        grid=(t // tm,),
        in_specs=[pl.BlockSpec((tm, D_MODEL), row), _const_spec((1, D_MODEL)),
                  _const_spec((D_MODEL, 2 * D_FF)), _const_spec((D_FF, D_MODEL)),
                  _const_spec((1, D_MODEL))],
        out_specs=pl.BlockSpec((tm, D_MODEL), row),
        compiler_params=_params(("parallel",)),
        name="ffn",
    )(x1, norm_ffn, w_in, w_out, norm_final)


def _rope_tables(seq_len):
    half = ROPE_DIM // 2
    inv = 1.0 / (ROPE_THETA ** (jnp.arange(0, ROPE_DIM, 2, dtype=F32) / ROPE_DIM))
    ang = jnp.arange(seq_len, dtype=F32)[:, None] * inv[None, :]
    cos, sin = jnp.cos(ang), jnp.sin(ang)
    pad = jnp.zeros((seq_len, DIFF_DH - ROPE_DIM), F32)
    cos_h = jnp.concatenate([cos, cos, pad + 1.0], axis=-1)
    sa_h = jnp.concatenate([jnp.zeros_like(sin), sin, pad], axis=-1)
    sb_h = jnp.concatenate([-sin, jnp.zeros_like(sin), pad], axis=-1)
    two = lambda a: jnp.concatenate([a, a], axis=-1)
    return two(cos_h), two(sa_h), two(sb_h)


def _trunk(x, mem, w, s5m):
    nb, seq_len, _ = x.shape
    nc = seq_len // S5_ROW
    x2 = x.reshape(nb * seq_len, D_MODEL)
    cos_t, sa_t, sb_t = _rope_tables(seq_len)
    u, q, k, v, xq = _inproj(x2, w["norm_mix"], w["w_a"], cos_t, sa_t, sb_t, seq_len)
    mk, mv = _memkv(mem.reshape(nb * N_MEM, D_MODEL), w["norm_mem"], w["w_mem_kv"])

    u_rows = (u.reshape(nb, nc, S5_ROW, S5_GROUPS, S5_GROUP).transpose(3, 1, 0, 2, 4)
              .reshape(S5_GROUPS, nc * nb, S5_LANES).astype(BF16))
    y_rows = _s5_core(u_rows, *s5m, nb=nb, nc=nc)
    ys = (y_rows.reshape(S5_GROUPS, nc, nb, S5_ROW, S5_GROUP).transpose(2, 1, 3, 0, 4)
          .reshape(nb * seq_len, S5_WIDTH))

    yd = _diff_attention(q, k, v, w["lq1"], w["lk1"], w["lq2"], w["lk2"], w["subln"],
                         nb, seq_len)
    x1 = _merge(x2, u, ys, yd, xq, mk, mv, w["norm_mix"], w["w_g"], w["s5_d"], w["w_glu"],
                w["b_glu"], w["w_up"], w["w_out"], seq_len)
    out = _ffn(x1, w["norm_ffn"], w["w_ffn_in"], w["w_ffn_out"], w["norm_final"])
    return out.reshape(nb, seq_len, D_MODEL)


def kernel(x_prompt, x_sample, mem_prompt, mem_sample, norm_mix, norm_mem, w_in, s5_lambda_re, s5_lambda_im, s5_log_step, s5_b_re, s5_b_im, s5_c_re, s5_c_im, s5_d, s5_w_glu, s5_b_glu, diff_lambda_q1, diff_lambda_k1, diff_lambda_q2, diff_lambda_k2, diff_subln, w_mem_kv, w_up, w_out, norm_ffn, w_ffn_in, w_ffn_out, norm_final):
    l = 0
    w = dict(
        norm_mix=norm_mix[l][None, :], norm_mem=norm_mem[l][None, :],
        w_a=w_in[l][:, :A_COLS].astype(BF16), w_g=w_in[l][:, A_COLS:].astype(BF16),
        s5_d=s5_d[l][None, :], w_glu=s5_w_glu[l].astype(BF16), b_glu=s5_b_glu[l][None, :],
        lq1=diff_lambda_q1[l][None, :], lk1=diff_lambda_k1[l][None, :],
        lq2=diff_lambda_q2[l][None, :], lk2=diff_lambda_k2[l][None, :],
        subln=diff_subln[l][None, :], w_mem_kv=w_mem_kv[l].astype(BF16),
        w_up=w_up[l].astype(BF16), w_out=w_out[l].astype(BF16),
        norm_ffn=norm_ffn[l][None, :], w_ffn_in=w_ffn_in[l].astype(BF16),
        w_ffn_out=w_ffn_out[l].astype(BF16), norm_final=norm_final[None, :],
    )
    s5m = _s5_matrices(s5_lambda_re[l], s5_lambda_im[l], s5_log_step[l],
                       s5_b_re[l], s5_b_im[l], s5_c_re[l], s5_c_im[l])
    return (_trunk(x_prompt, mem_prompt, w, s5m), _trunk(x_sample, mem_sample, w, s5m))
```

```python
import functools
import math

import jax
import jax.numpy as jnp
from jax import lax
from jax.experimental import pallas as pl
from jax.experimental.pallas import tpu as pltpu

F32 = jnp.float32
BF16 = jnp.bfloat16

D_MODEL = 1024
N_MEM = 256
EPS = 1e-6
S5_WIDTH = 512
S5_GROUP = 16
S5_GROUPS = 32
S5_STATE = 64
DIFF_HEADS = 4
DIFF_DH = 64
DIFF_VDH = 128
X_HEADS = 4
X_DH = 128
BRANCH_WIDTH = 512
D_FF = 2816
ROPE_THETA = 500000.0
ROPE_DIM = 16
LAMBDA_INIT = 0.8 - 0.6 * math.exp(-0.3 * 0)
LOG2E = 1.4426950408889634
Q_SCALE = DIFF_DH ** -0.5 * LOG2E
A_COLS = 5 * 512
S5_ROW = 16
S5_LANES = S5_ROW * S5_GROUP
SEQ_BLK = 16
STEP_BLK = 32

VMEM_LIMIT = 56 * 1024 * 1024


def _const_spec(shape):
    nd = len(shape)
    return pl.BlockSpec(shape, lambda *_: (0,) * nd, pipeline_mode=pl.Buffered(1))


def _params(sem):
    return pltpu.CompilerParams(dimension_semantics=sem, vmem_limit_bytes=VMEM_LIMIT)


def _rms(x, g):
    return x * lax.rsqrt(jnp.mean(x * x, axis=-1, keepdims=True) + EPS) * g


def _block_transpose8(xs, lane_block):
    rolled = []
    for r in range(8):
        s = xs[(7 + r) % 8]
        for k in range(6, -1, -1):
            s = jnp.where(lane_block == k, xs[(k + r) % 8], s)
        rolled.append(s if r == 0 else pltpu.roll(s, 16 * r, 1))
    outs = []
    for k in range(8):
        o = rolled[(7 - k) % 8]
        for m in range(6, -1, -1):
            o = jnp.where(lane_block == m, rolled[(m - k) % 8], o)
        outs.append(o)
    return outs


def _step_rows(c, hf, m):
    return pl.ds(c * S5_ROW + hf * 8 + m, SEQ_BLK, stride=STEP_BLK)


def _inproj_kernel(x_ref, g_ref, w_ref, cos_ref, sa_ref, sb_ref,
                   u_ref, ur_ref, q_ref, k_ref, v_ref, xq_ref, us):
    rows = SEQ_BLK * STEP_BLK
    xn = _rms(x_ref[...].reshape(rows, D_MODEL), g_ref[...]).astype(BF16)
    h = jnp.dot(xn, w_ref[...], preferred_element_type=F32)
    u = h[:, 0:512]
    u_ref[...] = u.reshape(SEQ_BLK, STEP_BLK, 512)
    for col in range(4):
        us[col] = u[:, col * 128:(col + 1) * 128]
    tile = lambda t: jnp.broadcast_to(t[None], (SEQ_BLK, STEP_BLK, 128)).reshape(rows, 128)
    cos = tile(cos_ref[...])
    sa = tile(sa_ref[...])
    sb = tile(sb_ref[...])
    for j in range(4):
        lo = 128 * j
        tq = h[:, 512 + lo:512 + lo + 128]
        tk = h[:, 1024 + lo:1024 + lo + 128]
        rq = tq * cos + pltpu.roll(tq, 8, 1) * sa + pltpu.roll(tq, 120, 1) * sb
        rk = tk * cos + pltpu.roll(tk, 8, 1) * sa + pltpu.roll(tk, 120, 1) * sb
        q_ref[:, :, lo:lo + 128] = (rq * Q_SCALE).astype(BF16).reshape(SEQ_BLK, STEP_BLK, 128)
        k_ref[:, :, lo:lo + 128] = rk.astype(BF16).reshape(SEQ_BLK, STEP_BLK, 128)
    v_ref[...] = h[:, 1536:2048].astype(BF16).reshape(SEQ_BLK, STEP_BLK, 512)
    xq_ref[...] = h[:, 2048:2560].astype(BF16).reshape(SEQ_BLK, STEP_BLK, 512)

    lane_block = lax.broadcasted_iota(jnp.int32, (SEQ_BLK, 128), 1) // S5_GROUP
    for c in range(STEP_BLK // S5_ROW):
        for hf in range(2):
            for col in range(4):
                outs = _block_transpose8([us[col, _step_rows(c, hf, m), :] for m in range(8)],
                                         lane_block)
                for k in range(8):
                    ur_ref[8 * col + k, c, :, hf * 128:(hf + 1) * 128] = outs[k].astype(BF16)


def _tok_blk(width):
    return pl.BlockSpec((SEQ_BLK, STEP_BLK, width), lambda b, t: (b, t, 0))


def _s5_blk():
    return pl.BlockSpec((S5_GROUPS, STEP_BLK // S5_ROW, SEQ_BLK, S5_LANES),
                        lambda b, t: (0, t, b, 0))


def _inproj(x, norm_mix, w_a, cos_t, sa_t, sb_t):
    nb, seq_len, _ = x.shape
    nc = seq_len // S5_ROW
    tab = pl.BlockSpec((STEP_BLK, 128), lambda b, t: (t, 0))
    o512 = lambda dt: jax.ShapeDtypeStruct((nb, seq_len, 512), dt)
    outs = [o512(F32), jax.ShapeDtypeStruct((S5_GROUPS, nc, nb, S5_LANES), BF16)] + [o512(BF16)] * 4
    return pl.pallas_call(
        _inproj_kernel,
        out_shape=outs,
        grid=(nb // SEQ_BLK, seq_len // STEP_BLK),
        in_specs=[_tok_blk(D_MODEL), _const_spec((1, D_MODEL)), _const_spec((D_MODEL, A_COLS)),
                  tab, tab, tab],
        out_specs=[_tok_blk(512), _s5_blk()] + [_tok_blk(512)] * 4,
        scratch_shapes=[pltpu.VMEM((4, SEQ_BLK * STEP_BLK, 128), F32)],
        compiler_params=_params(("parallel", "parallel")),
        name="inproj",
    )(x, norm_mix, w_a, cos_t, sa_t, sb_t)


def _memkv_kernel(m_ref, g_ref, w_ref, mk_ref, mv_ref):
    mn = _rms(m_ref[...], g_ref[...]).astype(BF16)
    h = jnp.dot(mn, w_ref[...], preferred_element_type=F32)
    mk_ref[...] = h[:, 0:512].astype(BF16)
    mv_ref[...] = h[:, 512:1024].astype(BF16)


def _memkv(mem2, norm_mem, w_kv, tm=512):
    t = mem2.shape[0]
    row = lambda i: (i, 0)
    return pl.pallas_call(
        _memkv_kernel,
        out_shape=[jax.ShapeDtypeStruct((t, 512), BF16)] * 2,
        grid=(t // tm,),
        in_specs=[pl.BlockSpec((tm, D_MODEL), row), _const_spec((1, D_MODEL)),
                  _const_spec((D_MODEL, 1024))],
        out_specs=[pl.BlockSpec((tm, 512), row)] * 2,
        compiler_params=_params(("parallel",)),
        name="memkv",
    )(mem2, norm_mem, w_kv)


def _cexp(mag_arg, ang):
    m = jnp.exp(mag_arg)
    return m * jnp.cos(ang), m * jnp.sin(ang)


def _s5_param_kernel(lr_row, li_row, lr_col, li_col, ls_ref, bt_r, bt_i, ct_r, ct_i,
                     m1r_ref, m1i_ref, wr_ref, wi_ref, g2_ref, dr_ref, di_ref):
    s_idx = (lax.broadcasted_iota(jnp.int32, (S5_LANES, 1), 0) // S5_GROUP).astype(F32)
    i_idx = (lax.broadcasted_iota(jnp.int32, (1, S5_LANES), 1) // S5_GROUP).astype(F32)
    for d in range(2):
        step = jnp.exp(ls_ref[d])
        lr = lr_row[d]
        li = li_row[d]
        ar = lr * step
        ai = li * step
        lb_r, lb_i = _cexp(ar, ai)
        den = lr * lr + li * li
        nr = lb_r - 1.0
        cf_r = (nr * lr + lb_i * li) / den
        cf_i = (lb_i * lr - nr * li) / den
        btr = bt_r[d]
        bti = bt_i[d]
        bb_r = cf_r * btr - cf_i * bti
        bb_i = cf_r * bti + cf_i * btr
        e = (15.0 - s_idx) if d == 0 else s_idx
        pw_r, pw_i = _cexp(e * ar, e * ai)
        m1r_ref[d] = pw_r * bb_r - pw_i * bb_i
        m1i_ref[d] = pw_r * bb_i + pw_i * bb_r
        dr, di = _cexp(16.0 * ar, 16.0 * ai)
        dr_ref[d] = dr
        di_ref[d] = di

        acr = lr_col[d] * step
        aci = li_col[d] * step
        ctr = ct_r[d]
        cti = ct_i[d]
        jw = (i_idx + 1.0) if d == 0 else (16.0 - i_idx)
        qr, qi = _cexp(jw * acr, jw * aci)
        wr_ref[d] = ctr * qr - cti * qi
        wi_ref[d] = -(ctr * qi + cti * qr)
        gr, gi = _cexp(i_idx * acr, i_idx * aci)
        cl_r = ctr * gr - cti * gi
        cl_i = ctr * gi + cti * gr
        g2_ref[d] = (jnp.dot(bb_r[0:16], cl_r, preferred_element_type=F32,
                             precision=lax.Precision.HIGHEST)
                     - jnp.dot(bb_i[0:16], cl_i, preferred_element_type=F32,
                               precision=lax.Precision.HIGHEST))


def _s5_matrices(lam_re, lam_im, log_step, b_re, b_im, c_re, c_im):
    G = S5_GROUPS
    lr_row = lam_re.reshape(2, G, 1, S5_STATE)
    li_row = lam_im.reshape(2, G, 1, S5_STATE)
    lr_col = lam_re.reshape(2, G, S5_STATE, 1)
    li_col = lam_im.reshape(2, G, S5_STATE, 1)
    ls = log_step.reshape(2, G, 1, 1)
    bt = lambda b: jnp.tile(jnp.swapaxes(b, -1, -2), (1, 1, S5_ROW, 1))
    ct = lambda c: jnp.tile(jnp.swapaxes(c, -1, -2), (1, 1, 1, S5_ROW))
    blk = lambda a, b: pl.BlockSpec((2, None, a, b), lambda g: (0, g, 0, 0))
    sd = lambda a, b: jax.ShapeDtypeStruct((2, G, a, b), F32)
    m1r, m1i, wr, wi, g2, dr, di = pl.pallas_call(
        _s5_param_kernel,
        out_shape=[sd(256, 64), sd(256, 64), sd(64, 256), sd(64, 256), sd(16, 256),
                   sd(1, 64), sd(1, 64)],
        grid=(G,),
        in_specs=[blk(1, 64), blk(1, 64), blk(64, 1), blk(64, 1), blk(1, 1),
                  blk(256, 64), blk(256, 64), blk(64, 256), blk(64, 256)],
        out_specs=[blk(256, 64), blk(256, 64), blk(64, 256), blk(64, 256), blk(16, 256),
                   blk(1, 64), blk(1, 64)],
        compiler_params=_params(("parallel",)),
        name="s5_params",
    )(lr_row, li_row, lr_col, li_col, ls, bt(b_re), bt(b_im), ct(c_re), ct(c_im))

    m1 = jnp.concatenate([m1r[0], m1r[1], m1i[0], m1i[1]], axis=-1)
    wa = jnp.concatenate([wr[0], wr[1]], axis=1)
    wb = jnp.concatenate([wi[0], wi[1]], axis=1)
    kf = g2[0].reshape(G, 16, 16, 16)
    kb = g2[1].reshape(G, 16, 16, 16)
    ribbon = jnp.concatenate([kb[:, :, :0:-1], kf[:, :, 0:1] + kb[:, :, 0:1], kf[:, :, 1:]],
                             axis=2).reshape(G, 16, 31 * 16)
    tz = jnp.stack([ribbon[:, :, 16 * (15 - s):16 * (15 - s) + 256] for s in range(16)],
                   axis=1).reshape(G, 256, 256)
    m2 = jnp.concatenate([tz, wa, wb], axis=1)
    dec_r = jnp.concatenate([dr[0], dr[1]], axis=-1)
    dec_i = jnp.concatenate([di[0], di[1]], axis=-1)
    return m1.astype(BF16), m2.astype(BF16), dec_r, dec_i


def _s5_kernel(u_ref, m1_ref, m2_ref, dr_ref, di_ref, y_ref, ea, eb, sa, sb, *, nb, nc):
    e = jnp.dot(u_ref[...], m1_ref[...], preferred_element_type=F32)
    ea[...] = e[:, 0:128]
    eb[...] = e[:, 128:256]
    ar = dr_ref[...]
    ai = di_ref[...]
    fwd = lax.broadcasted_iota(jnp.int32, (nb, 128), 1) < S5_STATE

    def scan(c, carry):
        s_re, s_im = carry
        r0 = pl.multiple_of(c * nb, nb)
        r1 = pl.multiple_of((nc - 1 - c) * nb, nb)
        sa[pl.ds(r0, nb), :] = s_re
        sb[pl.ds(r0, nb), :] = s_im
        in_re = jnp.where(fwd, ea[pl.ds(r0, nb), :], ea[pl.ds(r1, nb), :])
        in_im = jnp.where(fwd, eb[pl.ds(r0, nb), :], eb[pl.ds(r1, nb), :])
        return (ar * s_re - ai * s_im + in_re, ar * s_im + ai * s_re + in_im)

    zero = jnp.zeros((nb, 128), F32)
    lax.fori_loop(0, nc, scan, (zero, zero))

    def unflip(c, _):
        r0 = pl.multiple_of(c * nb, nb)
        r1 = pl.multiple_of((nc - 1 - c) * nb, nb)
        ea[pl.ds(r0, nb), :] = jnp.where(fwd, sa[pl.ds(r0, nb), :], sa[pl.ds(r1, nb), :])
        eb[pl.ds(r0, nb), :] = jnp.where(fwd, sb[pl.ds(r0, nb), :], sb[pl.ds(r1, nb), :])
        return 0

    lax.fori_loop(0, nc, unflip, 0)
    y = jnp.dot(u_ref[...], m2_ref[0:256, :], preferred_element_type=F32)
    y += jnp.dot(ea[...].astype(BF16), m2_ref[256:384, :], preferred_element_type=F32)
    y += jnp.dot(eb[...].astype(BF16), m2_ref[384:512, :], preferred_element_type=F32)
    y_ref[...] = y


def _s5_core(u_rows, m1, m2, dec_r, dec_i, nb, nc):
    m = nb * nc
    grp = lambda a, b: pl.BlockSpec((None, a, b), lambda g: (g, 0, 0))
    return pl.pallas_call(
        functools.partial(_s5_kernel, nb=nb, nc=nc),
        out_shape=jax.ShapeDtypeStruct((S5_GROUPS, m, S5_LANES), F32),
        grid=(S5_GROUPS,),
        in_specs=[grp(m, 256), grp(256, 256), grp(512, 256), grp(1, 128), grp(1, 128)],
        out_specs=grp(m, 256),
        scratch_shapes=[pltpu.VMEM((m, 128), F32)] * 4,
        compiler_params=_params(("parallel",)),
        name="s5_core",
    )(u_rows, m1, m2, dec_r, dec_i)


def _attn_kernel(q_ref, k_ref, v_ref, xq_ref, mk_ref, mv_ref, lq1, lk1, lq2, lk2, sg_ref,
                 yd_ref, yx_ref, vext):
    seq_len = k_ref.shape[0]

    @pl.when(pl.program_id(1) == 0)
    def _():
        for n in range(DIFF_HEADS):
            vext[n, :, 0:128] = v_ref[:, 128 * n:128 * n + 128]
            vext[n, :, 128:256] = jnp.ones((seq_len, 128), BF16)

    lam = (jnp.exp(jnp.sum(lq1[...] * lk1[...], axis=-1, keepdims=True))
           - jnp.exp(jnp.sum(lq2[...] * lk2[...], axis=-1, keepdims=True))
           + LAMBDA_INIT)
    first = lax.broadcasted_iota(jnp.int32, (1, 128), 1) < DIFF_DH
    zero = jnp.zeros((), BF16)
    for n in range(DIFF_HEADS):
        lo = 128 * n
        q = q_ref[:, lo:lo + 128]
        k = k_ref[:, lo:lo + 128]
        comps = []
        for c in range(2):
            qc = jnp.where(first if c == 0 else jnp.logical_not(first), q, zero)
            s = lax.dot_general(qc, k, (((1,), (1,)), ((), ())), preferred_element_type=F32)
            e = jnp.exp2(s - jnp.max(s, axis=-1, keepdims=True)).astype(BF16)
            pv = jnp.dot(e, vext[n], preferred_element_type=F32)
            comps.append(pv[:, 0:128] / pv[:, 128:256])
        o = comps[0] - lam * comps[1]
        o = _rms(o, sg_ref[...]) * (1.0 - LAMBDA_INIT)
        yd_ref[:, lo:lo + 128] = o.astype(BF16)

    for h in range(X_HEADS):
        lo = 128 * h
        s = lax.dot_general(xq_ref[:, lo:lo + 128], mk_ref[:, lo:lo + 128],
                            (((1,), (1,)), ((), ())), preferred_element_type=F32)
        s = s * (X_DH ** -0.5)
        e = jnp.exp(s - jnp.max(s, axis=-1, keepdims=True))
        p = e * (1.0 / jnp.sum(e, axis=-1, keepdims=True))
        yx_ref[:, lo:lo + 128] = jnp.dot(p.astype(BF16), mv_ref[:, lo:lo + 128],
                                         preferred_element_type=F32).astype(BF16)


def _attention(q, k, v, xq, mk, mv, lq1, lk1, lq2, lk2, subln, nbatch, seq_len, tq=512):
    nq = seq_len // tq
    vec = _const_spec((1, DIFF_DH))
    qblk = pl.BlockSpec((tq, 512), lambda b, i: (b * nq + i, 0))
    kv = pl.BlockSpec((seq_len, 512), lambda b, i: (b, 0))
    mem = pl.BlockSpec((N_MEM, 512), lambda b, i: (b, 0))
    return pl.pallas_call(
        _attn_kernel,
        out_shape=[jax.ShapeDtypeStruct(q.shape, BF16)] * 2,
        grid=(nbatch, nq),
        in_specs=[qblk, kv, kv, qblk, mem, mem, vec, vec, vec, vec, _const_spec((1, DIFF_VDH))],
        out_specs=[qblk, qblk],
        scratch_shapes=[pltpu.VMEM((DIFF_HEADS, seq_len, 256), BF16)],
        compiler_params=_params(("parallel", "arbitrary")),
        name="attention",
    )(q, k, v, xq, mk, mv, lq1, lk1, lq2, lk2, subln)


def _merge_kernel(x_ref, u_ref, yr_ref, yd_ref, yx_ref,
                  gmix_ref, wg_ref, d_ref, wglu_ref, bglu_ref, wup_ref, wout_ref, o_ref, ys):
    rows = SEQ_BLK * STEP_BLK
    lane_block = lax.broadcasted_iota(jnp.int32, (SEQ_BLK, 128), 1) // S5_GROUP
    for c in range(STEP_BLK // S5_ROW):
        for hf in range(2):
            for col in range(4):
                outs = _block_transpose8(
                    [yr_ref[8 * col + k, c, :, hf * 128:(hf + 1) * 128] for k in range(8)],
                    lane_block)
                for m in range(8):
                    ys[col, _step_rows(c, hf, m), :] = outs[m]

    x = x_ref[...].reshape(rows, D_MODEL)
    xn = _rms(x, gmix_ref[...]).astype(BF16)

    y = (jnp.concatenate([ys[col] for col in range(4)], axis=-1)
         + d_ref[...] * u_ref[...].reshape(rows, 512))
    z = jax.nn.gelu(y)
    zg = jnp.dot(z.astype(BF16), wglu_ref[...], preferred_element_type=F32) + bglu_ref[...]
    y_s5 = z * jax.nn.sigmoid(zg)

    branches = (y_s5.astype(BF16), yd_ref[...].reshape(rows, 512), yx_ref[...].reshape(rows, 512))
    merged = None
    for n in range(3):
        g = jnp.dot(xn, wg_ref[:, n * D_MODEL:(n + 1) * D_MODEL], preferred_element_type=F32)
        up = jnp.dot(branches[n], wup_ref[n], preferred_element_type=F32)
        term = jax.nn.sigmoid(g) * up
        merged = term if merged is None else merged + term
    out = x + jnp.dot(merged.astype(BF16), wout_ref[...], preferred_element_type=F32)
    o_ref[...] = out.reshape(SEQ_BLK, STEP_BLK, D_MODEL)


def _merge(x, u, y_rows, yd, yx, norm_mix, w_g, s5_d, w_glu, b_glu, w_up, w_out):
    nb, seq_len, _ = x.shape
    return pl.pallas_call(
        _merge_kernel,
        out_shape=jax.ShapeDtypeStruct((nb, seq_len, D_MODEL), F32),
        grid=(nb // SEQ_BLK, seq_len // STEP_BLK),
        in_specs=[_tok_blk(D_MODEL), _tok_blk(512), _s5_blk(), _tok_blk(512), _tok_blk(512),
                  _const_spec((1, D_MODEL)), _const_spec((D_MODEL, 3 * D_MODEL)),
                  _const_spec((1, 512)), _const_spec((512, 512)), _const_spec((1, 512)),
                  _const_spec((3, BRANCH_WIDTH, D_MODEL)), _const_spec((D_MODEL, D_MODEL))],
        out_specs=_tok_blk(D_MODEL),
        scratch_shapes=[pltpu.VMEM((4, SEQ_BLK * STEP_BLK, 128), F32)],
        compiler_params=_params(("parallel", "parallel")),
        name="merge",
    )(x, u, y_rows, yd, yx, norm_mix, w_g, s5_d, w_glu, b_glu, w_up, w_out)


def _ffn_kernel(x_ref, g_ref, win_ref, wout_ref, gf_ref, o_ref):
    x = x_ref[...]
    xn = _rms(x, g_ref[...]).astype(BF16)
    gate = jnp.dot(xn, win_ref[:, 0:D_FF], preferred_element_type=F32)
    up = jnp.dot(xn, win_ref[:, D_FF:2 * D_FF], preferred_element_type=F32)
    act = (jax.nn.silu(gate) * up).astype(BF16)
    x = x + jnp.dot(act, wout_ref[...], preferred_element_type=F32)
    o_ref[...] = _rms(x, gf_ref[...])


def _ffn(x1, norm_ffn, w_in, w_out, norm_final, tm=256):
    t = x1.shape[0]
    row = lambda i: (i, 0)
    return pl.pallas_call(
        _ffn_kernel,
        out_shape=jax.ShapeDtypeStruct((t, D_MODEL), F32),
        grid=(t // tm,),
        in_specs=[pl.BlockSpec((tm, D_MODEL), row), _const_spec((1, D_MODEL)),
                  _const_spec((D_MODEL, 2 * D_FF)), _const_spec((D_FF, D_MODEL)),
                  _const_spec((1, D_MODEL))],
        out_specs=pl.BlockSpec((tm, D_MODEL), row),
        compiler_params=_params(("parallel",)),
        name="ffn",
    )(x1, norm_ffn, w_in, w_out, norm_final)


def _rope_tables(seq_len):
    half = ROPE_DIM // 2
    inv = 1.0 / (ROPE_THETA ** (jnp.arange(0, ROPE_DIM, 2, dtype=F32) / ROPE_DIM))
    ang = jnp.arange(seq_len, dtype=F32)[:, None] * inv[None, :]
    cos, sin = jnp.cos(ang), jnp.sin(ang)
    pad = jnp.zeros((seq_len, DIFF_DH - ROPE_DIM), F32)
    cos_h = jnp.concatenate([cos, cos, pad + 1.0], axis=-1)
    sa_h = jnp.concatenate([jnp.zeros_like(sin), sin, pad], axis=-1)
    sb_h = jnp.concatenate([-sin, jnp.zeros_like(sin), pad], axis=-1)
    two = lambda a: jnp.concatenate([a, a], axis=-1)
    return two(cos_h), two(sa_h), two(sb_h)


def _trunk(x, mem, w, s5m):
    nb, seq_len, _ = x.shape
    nc = seq_len // S5_ROW
    t = nb * seq_len
    cos_t, sa_t, sb_t = _rope_tables(seq_len)
    u, u_rows, q, k, v, xq = _inproj(x, w["norm_mix"], w["w_a"], cos_t, sa_t, sb_t)
    mk, mv = _memkv(mem.reshape(nb * N_MEM, D_MODEL), w["norm_mem"], w["w_mem_kv"])
    y_rows = _s5_core(u_rows.reshape(S5_GROUPS, nc * nb, S5_LANES), *s5m, nb=nb, nc=nc)
    flat = lambda a: a.reshape(t, 512)
    yd, yx = _attention(flat(q), flat(k), flat(v), flat(xq), mk, mv,
                        w["lq1"], w["lk1"], w["lq2"], w["lk2"], w["subln"], nb, seq_len)
    seq = lambda a: a.reshape(nb, seq_len, 512)
    x1 = _merge(x, u, y_rows.reshape(S5_GROUPS, nc, nb, S5_LANES), seq(yd), seq(yx),
                w["norm_mix"], w["w_g"], w["s5_d"], w["w_glu"], w["b_glu"], w["w_up"], w["w_out"])
    out = _ffn(x1.reshape(t, D_MODEL), w["norm_ffn"], w["w_ffn_in"], w["w_ffn_out"],
               w["norm_final"])
    return out.reshape(nb, seq_len, D_MODEL)


def kernel(x_prompt, x_sample, mem_prompt, mem_sample, norm_mix, norm_mem, w_in, s5_lambda_re, s5_lambda_im, s5_log_step, s5_b_re, s5_b_im, s5_c_re, s5_c_im, s5_d, s5_w_glu, s5_b_glu, diff_lambda_q1, diff_lambda_k1, diff_lambda_q2, diff_lambda_k2, diff_subln, w_mem_kv, w_up, w_out, norm_ffn, w_ffn_in, w_ffn_out, norm_final):
    l = 0
    w = dict(
        norm_mix=norm_mix[l][None, :], norm_mem=norm_mem[l][None, :],
        w_a=w_in[l][:, :A_COLS].astype(BF16), w_g=w_in[l][:, A_COLS:].astype(BF16),
        s5_d=s5_d[l][None, :], w_glu=s5_w_glu[l].astype(BF16), b_glu=s5_b_glu[l][None, :],
        lq1=diff_lambda_q1[l][None, :], lk1=diff_lambda_k1[l][None, :],
        lq2=diff_lambda_q2[l][None, :], lk2=diff_lambda_k2[l][None, :],
        subln=diff_subln[l][None, :], w_mem_kv=w_mem_kv[l].astype(BF16),
        w_up=w_up[l].astype(BF16), w_out=w_out[l].astype(BF16),
        norm_ffn=norm_ffn[l][None, :], w_ffn_in=w_ffn_in[l].astype(BF16),
        w_ffn_out=w_ffn_out[l].astype(BF16), norm_final=norm_final[None, :],
    )
    s5m = _s5_matrices(s5_lambda_re[l], s5_lambda_im[l], s5_log_step[l],
                       s5_b_re[l], s5_b_im[l], s5_c_re[l], s5_c_im[l])
    return (_trunk(x_prompt, mem_prompt, w, s5m), _trunk(x_sample, mem_sample, w, s5m))
```

```python
import functools
import math

import jax
import jax.numpy as jnp
from jax import lax
from jax.experimental import pallas as pl
from jax.experimental.pallas import tpu as pltpu

F32 = jnp.float32
BF16 = jnp.bfloat16

D_MODEL = 1024
N_MEM = 256
EPS = 1e-6
S5_WIDTH = 512
S5_GROUP = 16
S5_GROUPS = 32
S5_STATE = 64
DIFF_HEADS = 4
DIFF_DH = 64
DIFF_VDH = 128
X_HEADS = 4
X_DH = 128
BRANCH_WIDTH = 512
D_FF = 2816
ROPE_THETA = 500000.0
ROPE_DIM = 16
LAMBDA_INIT = 0.8 - 0.6 * math.exp(-0.3 * 0)
LOG2E = 1.4426950408889634
Q_SCALE = DIFF_DH ** -0.5 * LOG2E
A_COLS = 5 * 512
S5_ROW = 16
S5_LANES = S5_ROW * S5_GROUP
SEQ_BLK = 16
STEP_BLK = 32

VMEM_LIMIT = 56 * 1024 * 1024


def _const_spec(shape):
    nd = len(shape)
    return pl.BlockSpec(shape, lambda *_: (0,) * nd, pipeline_mode=pl.Buffered(1))


def _params(sem):
    return pltpu.CompilerParams(dimension_semantics=sem, vmem_limit_bytes=VMEM_LIMIT)


def _rms(x, g):
    return x * lax.rsqrt(jnp.mean(x * x, axis=-1, keepdims=True) + EPS) * g


def _block_transpose8(xs, lane_block):
    rolled = []
    for r in range(8):
        s = xs[(7 + r) % 8]
        for k in range(6, -1, -1):
            s = jnp.where(lane_block == k, xs[(k + r) % 8], s)
        rolled.append(s if r == 0 else pltpu.roll(s, 16 * r, 1))
    outs = []
    for k in range(8):
        o = rolled[(7 - k) % 8]
        for m in range(6, -1, -1):
            o = jnp.where(lane_block == m, rolled[(m - k) % 8], o)
        outs.append(o)
    return outs


def _inproj_kernel(x_ref, g_ref, w_ref, cos_ref, sa_ref, sb_ref,
                   u_ref, ur_ref, q_ref, k_ref, v_ref, xq_ref, us):
    rows = SEQ_BLK * S5_ROW
    lane_block = lax.broadcasted_iota(jnp.int32, (SEQ_BLK, 128), 1) // S5_GROUP
    blk3 = lambda a, w: a.reshape(SEQ_BLK, S5_ROW, w)
    for c in range(STEP_BLK // S5_ROW):
        ts = slice(c * S5_ROW, (c + 1) * S5_ROW)
        xn = _rms(x_ref[:, ts, :].reshape(rows, D_MODEL), g_ref[...]).astype(BF16)
        h = jnp.dot(xn, w_ref[...], preferred_element_type=F32)
        u = h[:, 0:512]
        u_ref[:, ts, :] = blk3(u, 512)
        for col in range(4):
            us[c, col] = u[:, col * 128:(col + 1) * 128]

        for hf in range(2):
            for col in range(4):
                outs = _block_transpose8(
                    [us[c, col, pl.ds(hf * 8 + m, SEQ_BLK, stride=S5_ROW), :] for m in range(8)],
                    lane_block)
                for k in range(8):
                    ur_ref[8 * col + k, c, :, hf * 128:(hf + 1) * 128] = outs[k].astype(BF16)

        tile = lambda t: jnp.broadcast_to(t[None], (SEQ_BLK, S5_ROW, 128)).reshape(rows, 128)
        cos = tile(cos_ref[ts, :])
        sa = tile(sa_ref[ts, :])
        sb = tile(sb_ref[ts, :])
        for j in range(4):
            lo = 128 * j
            tq = h[:, 512 + lo:512 + lo + 128]
            tk = h[:, 1024 + lo:1024 + lo + 128]
            rq = tq * cos + pltpu.roll(tq, 8, 1) * sa + pltpu.roll(tq, 120, 1) * sb
            rk = tk * cos + pltpu.roll(tk, 8, 1) * sa + pltpu.roll(tk, 120, 1) * sb
            q_ref[:, ts, lo:lo + 128] = blk3((rq * Q_SCALE).astype(BF16), 128)
            k_ref[:, ts, lo:lo + 128] = blk3(rk.astype(BF16), 128)
        v_ref[:, ts, :] = blk3(h[:, 1536:2048].astype(BF16), 512)
        xq_ref[:, ts, :] = blk3(h[:, 2048:2560].astype(BF16), 512)


def _tok_blk(width):
    return pl.BlockSpec((SEQ_BLK, STEP_BLK, width), lambda b, t: (b, t, 0))


def _s5_blk():
    return pl.BlockSpec((S5_GROUPS, STEP_BLK // S5_ROW, SEQ_BLK, S5_LANES),
                        lambda b, t: (0, t, b, 0))


def _inproj(x, norm_mix, w_a, cos_t, sa_t, sb_t):
    nb, seq_len, _ = x.shape
    nc = seq_len // S5_ROW
    tab = pl.BlockSpec((STEP_BLK, 128), lambda b, t: (t, 0))
    o512 = lambda dt: jax.ShapeDtypeStruct((nb, seq_len, 512), dt)
    outs = [o512(F32), jax.ShapeDtypeStruct((S5_GROUPS, nc, nb, S5_LANES), BF16)] + [o512(BF16)] * 4
    return pl.pallas_call(
        _inproj_kernel,
        out_shape=outs,
        grid=(nb // SEQ_BLK, seq_len // STEP_BLK),
        in_specs=[_tok_blk(D_MODEL), _const_spec((1, D_MODEL)), _const_spec((D_MODEL, A_COLS)),
                  tab, tab, tab],
        out_specs=[_tok_blk(512), _s5_blk()] + [_tok_blk(512)] * 4,
        scratch_shapes=[pltpu.VMEM((STEP_BLK // S5_ROW, 4, SEQ_BLK * S5_ROW, 128), F32)],
        compiler_params=_params(("parallel", "parallel")),
        name="inproj",
    )(x, norm_mix, w_a, cos_t, sa_t, sb_t)


def _memkv_kernel(m_ref, g_ref, w_ref, mk_ref, mv_ref):
    mn = _rms(m_ref[...], g_ref[...]).astype(BF16)
    h = jnp.dot(mn, w_ref[...], preferred_element_type=F32)
    mk_ref[...] = h[:, 0:512].astype(BF16)
    mv_ref[...] = h[:, 512:1024].astype(BF16)


def _memkv(mem2, norm_mem, w_kv, tm=512):
    t = mem2.shape[0]
    row = lambda i: (i, 0)
    return pl.pallas_call(
        _memkv_kernel,
        out_shape=[jax.ShapeDtypeStruct((t, 512), BF16)] * 2,
        grid=(t // tm,),
        in_specs=[pl.BlockSpec((tm, D_MODEL), row), _const_spec((1, D_MODEL)),
                  _const_spec((D_MODEL, 1024))],
        out_specs=[pl.BlockSpec((tm, 512), row)] * 2,
        compiler_params=_params(("parallel",)),
        name="memkv",
    )(mem2, norm_mem, w_kv)


def _cexp(mag_arg, ang):
    m = jnp.exp(mag_arg)
    return m * jnp.cos(ang), m * jnp.sin(ang)


def _s5_param_kernel(lr_row, li_row, lr_col, li_col, ls_ref, bt_r, bt_i, ct_r, ct_i,
                     m1r_ref, m1i_ref, wr_ref, wi_ref, g2_ref, dr_ref, di_ref):
    s_idx = (lax.broadcasted_iota(jnp.int32, (S5_LANES, 1), 0) // S5_GROUP).astype(F32)
    i_idx = (lax.broadcasted_iota(jnp.int32, (1, S5_LANES), 1) // S5_GROUP).astype(F32)
    for d in range(2):
        step = jnp.exp(ls_ref[d])
        lr = lr_row[d]
        li = li_row[d]
        ar = lr * step
        ai = li * step
        lb_r, lb_i = _cexp(ar, ai)
        den = lr * lr + li * li
        nr = lb_r - 1.0
        cf_r = (nr * lr + lb_i * li) / den
        cf_i = (lb_i * lr - nr * li) / den
        btr = bt_r[d]
        bti = bt_i[d]
        bb_r = cf_r * btr - cf_i * bti
        bb_i = cf_r * bti + cf_i * btr
        e = (15.0 - s_idx) if d == 0 else s_idx
        pw_r, pw_i = _cexp(e * ar, e * ai)
        m1r_ref[d] = pw_r * bb_r - pw_i * bb_i
        m1i_ref[d] = pw_r * bb_i + pw_i * bb_r
        dr, di = _cexp(16.0 * ar, 16.0 * ai)
        dr_ref[d] = dr
        di_ref[d] = di

        acr = lr_col[d] * step
        aci = li_col[d] * step
        ctr = ct_r[d]
        cti = ct_i[d]
        jw = (i_idx + 1.0) if d == 0 else (16.0 - i_idx)
        qr, qi = _cexp(jw * acr, jw * aci)
        wr_ref[d] = ctr * qr - cti * qi
        wi_ref[d] = -(ctr * qi + cti * qr)
        gr, gi = _cexp(i_idx * acr, i_idx * aci)
        cl_r = ctr * gr - cti * gi
        cl_i = ctr * gi + cti * gr
        g2_ref[d] = (jnp.dot(bb_r[0:16], cl_r, preferred_element_type=F32,
                             precision=lax.Precision.HIGHEST)
                     - jnp.dot(bb_i[0:16], cl_i, preferred_element_type=F32,
                               precision=lax.Precision.HIGHEST))


def _s5_matrices(lam_re, lam_im, log_step, b_re, b_im, c_re, c_im):
    G = S5_GROUPS
    lr_row = lam_re.reshape(2, G, 1, S5_STATE)
    li_row = lam_im.reshape(2, G, 1, S5_STATE)
    lr_col = lam_re.reshape(2, G, S5_STATE, 1)
    li_col = lam_im.reshape(2, G, S5_STATE, 1)
    ls = log_step.reshape(2, G, 1, 1)
    bt = lambda b: jnp.tile(jnp.swapaxes(b, -1, -2), (1, 1, S5_ROW, 1))
    ct = lambda c: jnp.tile(jnp.swapaxes(c, -1, -2), (1, 1, 1, S5_ROW))
    blk = lambda a, b: pl.BlockSpec((2, None, a, b), lambda g: (0, g, 0, 0))
    sd = lambda a, b: jax.ShapeDtypeStruct((2, G, a, b), F32)
    m1r, m1i, wr, wi, g2, dr, di = pl.pallas_call(
        _s5_param_kernel,
        out_shape=[sd(256, 64), sd(256, 64), sd(64, 256), sd(64, 256), sd(16, 256),
                   sd(1, 64), sd(1, 64)],
        grid=(G,),
        in_specs=[blk(1, 64), blk(1, 64), blk(64, 1), blk(64, 1), blk(1, 1),
                  blk(256, 64), blk(256, 64), blk(64, 256), blk(64, 256)],
        out_specs=[blk(256, 64), blk(256, 64), blk(64, 256), blk(64, 256), blk(16, 256),
                   blk(1, 64), blk(1, 64)],
        compiler_params=_params(("parallel",)),
        name="s5_params",
    )(lr_row, li_row, lr_col, li_col, ls, bt(b_re), bt(b_im), ct(c_re), ct(c_im))

    m1 = jnp.concatenate([m1r[0], m1r[1], m1i[0], m1i[1]], axis=-1)
    wa = jnp.concatenate([wr[0], wr[1]], axis=1)
    wb = jnp.concatenate([wi[0], wi[1]], axis=1)
    kf = g2[0].reshape(G, 16, 16, 16)
    kb = g2[1].reshape(G, 16, 16, 16)
    ribbon = jnp.concatenate([kb[:, :, :0:-1], kf[:, :, 0:1] + kb[:, :, 0:1], kf[:, :, 1:]],
                             axis=2).reshape(G, 16, 31 * 16)
    tz = jnp.stack([ribbon[:, :, 16 * (15 - s):16 * (15 - s) + 256] for s in range(16)],
                   axis=1).reshape(G, 256, 256)
    m2 = jnp.concatenate([tz, wa, wb], axis=1)
    dec_r = jnp.concatenate([dr[0], dr[1]], axis=-1)
    dec_i = jnp.concatenate([di[0], di[1]], axis=-1)
    return m1.astype(BF16), m2.astype(BF16), dec_r, dec_i


def _s5_kernel(u_ref, m1_ref, m2_ref, dr_ref, di_ref, y_ref, ea, eb, sa, sb, *, nb, nc):
    e = jnp.dot(u_ref[...], m1_ref[...], preferred_element_type=F32)
    ea[...] = e[:, 0:128]
    eb[...] = e[:, 128:256]
    ar = dr_ref[...]
    ai = di_ref[...]
    fwd = lax.broadcasted_iota(jnp.int32, (nb, 128), 1) < S5_STATE

    def scan(c, carry):
        s_re, s_im = carry
        r0 = pl.multiple_of(c * nb, nb)
        r1 = pl.multiple_of((nc - 1 - c) * nb, nb)
        sa[pl.ds(r0, nb), :] = s_re
        sb[pl.ds(r0, nb), :] = s_im
        in_re = jnp.where(fwd, ea[pl.ds(r0, nb), :], ea[pl.ds(r1, nb), :])
        in_im = jnp.where(fwd, eb[pl.ds(r0, nb), :], eb[pl.ds(r1, nb), :])
        return (ar * s_re - ai * s_im + in_re, ar * s_im + ai * s_re + in_im)

    zero = jnp.zeros((nb, 128), F32)
    lax.fori_loop(0, nc, scan, (zero, zero))

    def unflip(c, _):
        r0 = pl.multiple_of(c * nb, nb)
        r1 = pl.multiple_of((nc - 1 - c) * nb, nb)
        ea[pl.ds(r0, nb), :] = jnp.where(fwd, sa[pl.ds(r0, nb), :], sa[pl.ds(r1, nb), :])
        eb[pl.ds(r0, nb), :] = jnp.where(fwd, sb[pl.ds(r0, nb), :], sb[pl.ds(r1, nb), :])
        return 0

    lax.fori_loop(0, nc, unflip, 0)
    y = jnp.dot(u_ref[...], m2_ref[0:256, :], preferred_element_type=F32)
    y += jnp.dot(ea[...].astype(BF16), m2_ref[256:384, :], preferred_element_type=F32)
    y += jnp.dot(eb[...].astype(BF16), m2_ref[384:512, :], preferred_element_type=F32)
    y_ref[...] = y


def _s5_core(u_rows, m1, m2, dec_r, dec_i, nb, nc):
    m = nb * nc
    grp = lambda a, b: pl.BlockSpec((None, a, b), lambda g: (g, 0, 0))
    return pl.pallas_call(
        functools.partial(_s5_kernel, nb=nb, nc=nc),
        out_shape=jax.ShapeDtypeStruct((S5_GROUPS, m, S5_LANES), F32),
        grid=(S5_GROUPS,),
        in_specs=[grp(m, 256), grp(256, 256), grp(512, 256), grp(1, 128), grp(1, 128)],
        out_specs=grp(m, 256),
        scratch_shapes=[pltpu.VMEM((m, 128), F32)] * 4,
        compiler_params=_params(("parallel",)),
        name="s5_core",
    )(u_rows, m1, m2, dec_r, dec_i)


def _attn_kernel(q_ref, k_ref, v_ref, xq_ref, mk_ref, mv_ref, lq1, lk1, lq2, lk2, sg_ref,
                 yd_ref, yx_ref, vext):
    seq_len = k_ref.shape[0]

    @pl.when(pl.program_id(1) == 0)
    def _():
        for n in range(DIFF_HEADS):
            vext[n, :, 0:128] = v_ref[:, 128 * n:128 * n + 128]
            vext[n, :, 128:256] = jnp.ones((seq_len, 128), BF16)

    lam = (jnp.exp(jnp.sum(lq1[...] * lk1[...], axis=-1, keepdims=True))
           - jnp.exp(jnp.sum(lq2[...] * lk2[...], axis=-1, keepdims=True))
           + LAMBDA_INIT)
    first = lax.broadcasted_iota(jnp.int32, (1, 128), 1) < DIFF_DH
    zero = jnp.zeros((), BF16)
    for n in range(DIFF_HEADS):
        lo = 128 * n
        q = q_ref[:, lo:lo + 128]
        k = k_ref[:, lo:lo + 128]
        comps = []
        for c in range(2):
            qc = jnp.where(first if c == 0 else jnp.logical_not(first), q, zero)
            s = lax.dot_general(qc, k, (((1,), (1,)), ((), ())), preferred_element_type=F32)
            e = jnp.exp2(s - jnp.max(s, axis=-1, keepdims=True)).astype(BF16)
            pv = jnp.dot(e, vext[n], preferred_element_type=F32)
            comps.append(pv[:, 0:128] / pv[:, 128:256])
        o = comps[0] - lam * comps[1]
        o = _rms(o, sg_ref[...]) * (1.0 - LAMBDA_INIT)
        yd_ref[:, lo:lo + 128] = o.astype(BF16)

    for h in range(X_HEADS):
        lo = 128 * h
        s = lax.dot_general(xq_ref[:, lo:lo + 128], mk_ref[:, lo:lo + 128],
                            (((1,), (1,)), ((), ())), preferred_element_type=F32)
        s = s * (X_DH ** -0.5)
        e = jnp.exp(s - jnp.max(s, axis=-1, keepdims=True))
        p = e * (1.0 / jnp.sum(e, axis=-1, keepdims=True))
        yx_ref[:, lo:lo + 128] = jnp.dot(p.astype(BF16), mv_ref[:, lo:lo + 128],
                                         preferred_element_type=F32).astype(BF16)


def _attention(q, k, v, xq, mk, mv, lq1, lk1, lq2, lk2, subln, nbatch, seq_len, tq=512):
    nq = seq_len // tq
    vec = _const_spec((1, DIFF_DH))
    qblk = pl.BlockSpec((tq, 512), lambda b, i: (b * nq + i, 0))
    kv = pl.BlockSpec((seq_len, 512), lambda b, i: (b, 0))
    mem = pl.BlockSpec((N_MEM, 512), lambda b, i: (b, 0))
    return pl.pallas_call(
        _attn_kernel,
        out_shape=[jax.ShapeDtypeStruct(q.shape, BF16)] * 2,
        grid=(nbatch, nq),
        in_specs=[qblk, kv, kv, qblk, mem, mem, vec, vec, vec, vec, _const_spec((1, DIFF_VDH))],
        out_specs=[qblk, qblk],
        scratch_shapes=[pltpu.VMEM((DIFF_HEADS, seq_len, 256), BF16)],
        compiler_params=_params(("parallel", "arbitrary")),
        name="attention",
    )(q, k, v, xq, mk, mv, lq1, lk1, lq2, lk2, subln)


def _merge_kernel(x_ref, u_ref, yr_ref, yd_ref, yx_ref,
                  gmix_ref, wg_ref, d_ref, wglu_ref, bglu_ref, wup_ref, wout_ref, o_ref, ys):
    rows = SEQ_BLK * S5_ROW
    lane_block = lax.broadcasted_iota(jnp.int32, (SEQ_BLK, 128), 1) // S5_GROUP
    for c in range(STEP_BLK // S5_ROW):
        ts = slice(c * S5_ROW, (c + 1) * S5_ROW)
        for hf in range(2):
            for col in range(4):
                outs = _block_transpose8(
                    [yr_ref[8 * col + k, c, :, hf * 128:(hf + 1) * 128] for k in range(8)],
                    lane_block)
                for m in range(8):
                    ys[c, col, pl.ds(hf * 8 + m, SEQ_BLK, stride=S5_ROW), :] = outs[m]

        x = x_ref[:, ts, :].reshape(rows, D_MODEL)
        xn = _rms(x, gmix_ref[...]).astype(BF16)

        y = (jnp.concatenate([ys[c, col] for col in range(4)], axis=-1)
             + d_ref[...] * u_ref[:, ts, :].reshape(rows, 512))
        z = jax.nn.gelu(y)
        zg = jnp.dot(z.astype(BF16), wglu_ref[...], preferred_element_type=F32) + bglu_ref[...]
        y_s5 = z * jax.nn.sigmoid(zg)

        branches = (y_s5.astype(BF16), yd_ref[:, ts, :].reshape(rows, 512),
                    yx_ref[:, ts, :].reshape(rows, 512))
        merged = None
        for n in range(3):
            g = jnp.dot(xn, wg_ref[:, n * D_MODEL:(n + 1) * D_MODEL],
                        preferred_element_type=F32)
            up = jnp.dot(branches[n], wup_ref[n], preferred_element_type=F32)
            term = jax.nn.sigmoid(g) * up
            merged = term if merged is None else merged + term
        out = x + jnp.dot(merged.astype(BF16), wout_ref[...], preferred_element_type=F32)
        o_ref[:, ts, :] = out.reshape(SEQ_BLK, S5_ROW, D_MODEL)


def _merge(x, u, y_rows, yd, yx, norm_mix, w_g, s5_d, w_glu, b_glu, w_up, w_out):
    nb, seq_len, _ = x.shape
    return pl.pallas_call(
        _merge_kernel,
        out_shape=jax.ShapeDtypeStruct((nb, seq_len, D_MODEL), F32),
        grid=(nb // SEQ_BLK, seq_len // STEP_BLK),
        in_specs=[_tok_blk(D_MODEL), _tok_blk(512), _s5_blk(), _tok_blk(512), _tok_blk(512),
                  _const_spec((1, D_MODEL)), _const_spec((D_MODEL, 3 * D_MODEL)),
                  _const_spec((1, 512)), _const_spec((512, 512)), _const_spec((1, 512)),
                  _const_spec((3, BRANCH_WIDTH, D_MODEL)), _const_spec((D_MODEL, D_MODEL))],
        out_specs=_tok_blk(D_MODEL),
        scratch_shapes=[pltpu.VMEM((STEP_BLK // S5_ROW, 4, SEQ_BLK * S5_ROW, 128), F32)],
        compiler_params=_params(("parallel", "parallel")),
        name="merge",
    )(x, u, y_rows, yd, yx, norm_mix, w_g, s5_d, w_glu, b_glu, w_up, w_out)


def _ffn_kernel(x_ref, g_ref, win_ref, wout_ref, gf_ref, o_ref):
    half = x_ref.shape[0] // 2
    for hb in range(2):
        rs = slice(hb * half, (hb + 1) * half)
        x = x_ref[rs, :]
        xn = _rms(x, g_ref[...]).astype(BF16)
        gate = jnp.dot(xn, win_ref[:, 0:D_FF], preferred_element_type=F32)
        up = jnp.dot(xn, win_ref[:, D_FF:2 * D_FF], preferred_element_type=F32)
        act = (jax.nn.silu(gate) * up).astype(BF16)
        x = x + jnp.dot(act, wout_ref[...], preferred_element_type=F32)
        o_ref[rs, :] = _rms(x, gf_ref[...])


def _ffn(x1, norm_ffn, w_in, w_out, norm_final, tm=512):
    t = x1.shape[0]
    row = lambda i: (i, 0)
    return pl.pallas_call(
        _ffn_kernel,
        out_shape=jax.ShapeDtypeStruct((t, D_MODEL), F32),
        grid=(t // tm,),
        in_specs=[pl.BlockSpec((tm, D_MODEL), row), _const_spec((1, D_MODEL)),
                  _const_spec((D_MODEL, 2 * D_FF)), _const_spec((D_FF, D_MODEL)),
                  _const_spec((1, D_MODEL))],
        out_specs=pl.BlockSpec((tm, D_MODEL), row),
        compiler_params=_params(("parallel",)),
        name="ffn",
    )(x1, norm_ffn, w_in, w_out, norm_final)


def _rope_tables(seq_len):
    half = ROPE_DIM // 2
    inv = 1.0 / (ROPE_THETA ** (jnp.arange(0, ROPE_DIM, 2, dtype=F32) / ROPE_DIM))
    ang = jnp.arange(seq_len, dtype=F32)[:, None] * inv[None, :]
    cos, sin = jnp.cos(ang), jnp.sin(ang)
    pad = jnp.zeros((seq_len, DIFF_DH - ROPE_DIM), F32)
    cos_h = jnp.concatenate([cos, cos, pad + 1.0], axis=-1)
    sa_h = jnp.concatenate([jnp.zeros_like(sin), sin, pad], axis=-1)
    sb_h = jnp.concatenate([-sin, jnp.zeros_like(sin), pad], axis=-1)
    two = lambda a: jnp.concatenate([a, a], axis=-1)
    return two(cos_h), two(sa_h), two(sb_h)


def _trunk(x, mem, w, s5m):
    nb, seq_len, _ = x.shape
    nc = seq_len // S5_ROW
    t = nb * seq_len
    cos_t, sa_t, sb_t = _rope_tables(seq_len)
    u, u_rows, q, k, v, xq = _inproj(x, w["norm_mix"], w["w_a"], cos_t, sa_t, sb_t)
    mk, mv = _memkv(mem.reshape(nb * N_MEM, D_MODEL), w["norm_mem"], w["w_mem_kv"])
    y_rows = _s5_core(u_rows.reshape(S5_GROUPS, nc * nb, S5_LANES), *s5m, nb=nb, nc=nc)
    flat = lambda a: a.reshape(t, 512)
    yd, yx = _attention(flat(q), flat(k), flat(v), flat(xq), mk, mv,
                        w["lq1"], w["lk1"], w["lq2"], w["lk2"], w["subln"], nb, seq_len)
    seq = lambda a: a.reshape(nb, seq_len, 512)
    x1 = _merge(x, u, y_rows.reshape(S5_GROUPS, nc, nb, S5_LANES), seq(yd), seq(yx),
                w["norm_mix"], w["w_g"], w["s5_d"], w["w_glu"], w["b_glu"], w["w_up"], w["w_out"])
    out = _ffn(x1.reshape(t, D_MODEL), w["norm_ffn"], w["w_ffn_in"], w["w_ffn_out"],
               w["norm_final"])
    return out.reshape(nb, seq_len, D_MODEL)


def kernel(x_prompt, x_sample, mem_prompt, mem_sample, norm_mix, norm_mem, w_in, s5_lambda_re, s5_lambda_im, s5_log_step, s5_b_re, s5_b_im, s5_c_re, s5_c_im, s5_d, s5_w_glu, s5_b_glu, diff_lambda_q1, diff_lambda_k1, diff_lambda_q2, diff_lambda_k2, diff_subln, w_mem_kv, w_up, w_out, norm_ffn, w_ffn_in, w_ffn_out, norm_final):
    l = 0
    w = dict(
        norm_mix=norm_mix[l][None, :], norm_mem=norm_mem[l][None, :],
        w_a=w_in[l][:, :A_COLS].astype(BF16), w_g=w_in[l][:, A_COLS:].astype(BF16),
        s5_d=s5_d[l][None, :], w_glu=s5_w_glu[l].astype(BF16), b_glu=s5_b_glu[l][None, :],
        lq1=diff_lambda_q1[l][None, :], lk1=diff_lambda_k1[l][None, :],
        lq2=diff_lambda_q2[l][None, :], lk2=diff_lambda_k2[l][None, :],
        subln=diff_subln[l][None, :], w_mem_kv=w_mem_kv[l].astype(BF16),
        w_up=w_up[l].astype(BF16), w_out=w_out[l].astype(BF16),
        norm_ffn=norm_ffn[l][None, :], w_ffn_in=w_ffn_in[l].astype(BF16),
        w_ffn_out=w_ffn_out[l].astype(BF16), norm_final=norm_final[None, :],
    )
    s5m = _s5_matrices(s5_lambda_re[l], s5_lambda_im[l], s5_log_step[l],
                       s5_b_re[l], s5_b_im[l], s5_c_re[l], s5_c_im[l])
    return (_trunk(x_prompt, mem_prompt, w, s5m), _trunk(x_sample, mem_sample, w, s5m))
```

```python
import functools
import math

import jax
import jax.numpy as jnp
from jax import lax
from jax.experimental import pallas as pl
from jax.experimental.pallas import tpu as pltpu

F32 = jnp.float32
BF16 = jnp.bfloat16

D_MODEL = 1024
N_MEM = 256
EPS = 1e-6
S5_WIDTH = 512
S5_GROUP = 16
S5_GROUPS = 32
S5_STATE = 64
DIFF_HEADS = 4
DIFF_DH = 64
DIFF_VDH = 128
X_HEADS = 4
X_DH = 128
BRANCH_WIDTH = 512
D_FF = 2816
ROPE_THETA = 500000.0
ROPE_DIM = 16
LAMBDA_INIT = 0.8 - 0.6 * math.exp(-0.3 * 0)
LOG2E = 1.4426950408889634
Q_SCALE = DIFF_DH ** -0.5 * LOG2E
A_COLS = 5 * 512
S5_ROW = 16
S5_LANES = S5_ROW * S5_GROUP
SEQ_BLK = 16
STEP_BLK = 64
FFN_ROWS = 256

VMEM_LIMIT = 56 * 1024 * 1024


def _const_spec(shape):
    nd = len(shape)
    return pl.BlockSpec(shape, lambda *_: (0,) * nd, pipeline_mode=pl.Buffered(1))


def _params(sem):
    return pltpu.CompilerParams(dimension_semantics=sem, vmem_limit_bytes=VMEM_LIMIT)


def _rms(x, g):
    return x * lax.rsqrt(jnp.mean(x * x, axis=-1, keepdims=True) + EPS) * g


def _block_transpose8(xs, lane_block):
    rolled = []
    for r in range(8):
        s = xs[(7 + r) % 8]
        for k in range(6, -1, -1):
            s = jnp.where(lane_block == k, xs[(k + r) % 8], s)
        rolled.append(s if r == 0 else pltpu.roll(s, 16 * r, 1))
    outs = []
    for k in range(8):
        o = rolled[(7 - k) % 8]
        for m in range(6, -1, -1):
            o = jnp.where(lane_block == m, rolled[(m - k) % 8], o)
        outs.append(o)
    return outs


def _inproj_kernel(x_ref, g_ref, w_ref, cos_ref, sa_ref, sb_ref,
                   u_ref, ur_ref, q_ref, k_ref, v_ref, xq_ref, us):
    rows = SEQ_BLK * S5_ROW
    lane_block = lax.broadcasted_iota(jnp.int32, (SEQ_BLK, 128), 1) // S5_GROUP
    blk3 = lambda a, w: a.reshape(SEQ_BLK, S5_ROW, w)
    for c in range(STEP_BLK // S5_ROW):
        ts = slice(c * S5_ROW, (c + 1) * S5_ROW)
        xn = _rms(x_ref[:, ts, :].reshape(rows, D_MODEL), g_ref[...]).astype(BF16)
        h = jnp.dot(xn, w_ref[...], preferred_element_type=F32)
        u = h[:, 0:512]
        u_ref[:, ts, :] = blk3(u, 512)
        for col in range(4):
            us[c, col] = u[:, col * 128:(col + 1) * 128]

        for hf in range(2):
            for col in range(4):
                outs = _block_transpose8(
                    [us[c, col, pl.ds(hf * 8 + m, SEQ_BLK, stride=S5_ROW), :] for m in range(8)],
                    lane_block)
                for k in range(8):
                    ur_ref[8 * col + k, c, :, hf * 128:(hf + 1) * 128] = outs[k].astype(BF16)

        tile = lambda t: jnp.broadcast_to(t[None], (SEQ_BLK, S5_ROW, 128)).reshape(rows, 128)
        cos = tile(cos_ref[ts, :])
        sa = tile(sa_ref[ts, :])
        sb = tile(sb_ref[ts, :])
        for j in range(4):
            lo = 128 * j
            tq = h[:, 512 + lo:512 + lo + 128]
            tk = h[:, 1024 + lo:1024 + lo + 128]
            rq = tq * cos + pltpu.roll(tq, 8, 1) * sa + pltpu.roll(tq, 120, 1) * sb
            rk = tk * cos + pltpu.roll(tk, 8, 1) * sa + pltpu.roll(tk, 120, 1) * sb
            q_ref[:, ts, lo:lo + 128] = blk3((rq * Q_SCALE).astype(BF16), 128)
            k_ref[:, ts, lo:lo + 128] = blk3(rk.astype(BF16), 128)
        v_ref[:, ts, :] = blk3(h[:, 1536:2048].astype(BF16), 512)
        xq_ref[:, ts, :] = blk3(h[:, 2048:2560].astype(BF16), 512)


def _tok_blk(width):
    return pl.BlockSpec((SEQ_BLK, STEP_BLK, width), lambda b, t: (b, t, 0))


def _s5_blk():
    return pl.BlockSpec((S5_GROUPS, STEP_BLK // S5_ROW, SEQ_BLK, S5_LANES),
                        lambda b, t: (0, t, b, 0))


def _inproj(x, norm_mix, w_a, cos_t, sa_t, sb_t):
    nb, seq_len, _ = x.shape
    nc = seq_len // S5_ROW
    tab = pl.BlockSpec((STEP_BLK, 128), lambda b, t: (t, 0))
    o512 = lambda dt: jax.ShapeDtypeStruct((nb, seq_len, 512), dt)
    outs = [o512(F32), jax.ShapeDtypeStruct((S5_GROUPS, nc, nb, S5_LANES), BF16)] + [o512(BF16)] * 4
    return pl.pallas_call(
        _inproj_kernel,
        out_shape=outs,
        grid=(nb // SEQ_BLK, seq_len // STEP_BLK),
        in_specs=[_tok_blk(D_MODEL), _const_spec((1, D_MODEL)), _const_spec((D_MODEL, A_COLS)),
                  tab, tab, tab],
        out_specs=[_tok_blk(512), _s5_blk()] + [_tok_blk(512)] * 4,
        scratch_shapes=[pltpu.VMEM((STEP_BLK // S5_ROW, 4, SEQ_BLK * S5_ROW, 128), F32)],
        compiler_params=_params(("parallel", "parallel")),
        name="inproj",
    )(x, norm_mix, w_a, cos_t, sa_t, sb_t)


def _memkv_kernel(m_ref, g_ref, w_ref, mk_ref, mv_ref):
    mn = _rms(m_ref[...], g_ref[...]).astype(BF16)
    h = jnp.dot(mn, w_ref[...], preferred_element_type=F32)
    mk_ref[...] = h[:, 0:512].astype(BF16)
    mv_ref[...] = h[:, 512:1024].astype(BF16)


def _memkv(mem2, norm_mem, w_kv, tm=512):
    t = mem2.shape[0]
    row = lambda i: (i, 0)
    return pl.pallas_call(
        _memkv_kernel,
        out_shape=[jax.ShapeDtypeStruct((t, 512), BF16)] * 2,
        grid=(t // tm,),
        in_specs=[pl.BlockSpec((tm, D_MODEL), row), _const_spec((1, D_MODEL)),
                  _const_spec((D_MODEL, 1024))],
        out_specs=[pl.BlockSpec((tm, 512), row)] * 2,
        compiler_params=_params(("parallel",)),
        name="memkv",
    )(mem2, norm_mem, w_kv)


def _cexp(mag_arg, ang):
    m = jnp.exp(mag_arg)
    return m * jnp.cos(ang), m * jnp.sin(ang)


def _s5_param_kernel(lr_row, li_row, lr_col, li_col, ls_ref, bt_r, bt_i, ct_r, ct_i,
                     m1r_ref, m1i_ref, wr_ref, wi_ref, g2_ref, dr_ref, di_ref):
    s_idx = (lax.broadcasted_iota(jnp.int32, (S5_LANES, 1), 0) // S5_GROUP).astype(F32)
    i_idx = (lax.broadcasted_iota(jnp.int32, (1, S5_LANES), 1) // S5_GROUP).astype(F32)
    for d in range(2):
        step = jnp.exp(ls_ref[d])
        lr = lr_row[d]
        li = li_row[d]
        ar = lr * step
        ai = li * step
        lb_r, lb_i = _cexp(ar, ai)
        den = lr * lr + li * li
        nr = lb_r - 1.0
        cf_r = (nr * lr + lb_i * li) / den
        cf_i = (lb_i * lr - nr * li) / den
        btr = bt_r[d]
        bti = bt_i[d]
        bb_r = cf_r * btr - cf_i * bti
        bb_i = cf_r * bti + cf_i * btr
        e = (15.0 - s_idx) if d == 0 else s_idx
        pw_r, pw_i = _cexp(e * ar, e * ai)
        m1r_ref[d] = pw_r * bb_r - pw_i * bb_i
        m1i_ref[d] = pw_r * bb_i + pw_i * bb_r
        dr, di = _cexp(16.0 * ar, 16.0 * ai)
        dr_ref[d] = dr
        di_ref[d] = di

        acr = lr_col[d] * step
        aci = li_col[d] * step
        ctr = ct_r[d]
        cti = ct_i[d]
        jw = (i_idx + 1.0) if d == 0 else (16.0 - i_idx)
        qr, qi = _cexp(jw * acr, jw * aci)
        wr_ref[d] = ctr * qr - cti * qi
        wi_ref[d] = -(ctr * qi + cti * qr)
        gr, gi = _cexp(i_idx * acr, i_idx * aci)
        cl_r = ctr * gr - cti * gi
        cl_i = ctr * gi + cti * gr
        g2_ref[d] = (jnp.dot(bb_r[0:16], cl_r, preferred_element_type=F32,
                             precision=lax.Precision.HIGHEST)
                     - jnp.dot(bb_i[0:16], cl_i, preferred_element_type=F32,
                               precision=lax.Precision.HIGHEST))


def _s5_matrices(lam_re, lam_im, log_step, b_re, b_im, c_re, c_im):
    G = S5_GROUPS
    lr_row = lam_re.reshape(2, G, 1, S5_STATE)
    li_row = lam_im.reshape(2, G, 1, S5_STATE)
    lr_col = lam_re.reshape(2, G, S5_STATE, 1)
    li_col = lam_im.reshape(2, G, S5_STATE, 1)
    ls = log_step.reshape(2, G, 1, 1)
    bt = lambda b: jnp.tile(jnp.swapaxes(b, -1, -2), (1, 1, S5_ROW, 1))
    ct = lambda c: jnp.tile(jnp.swapaxes(c, -1, -2), (1, 1, 1, S5_ROW))
    blk = lambda a, b: pl.BlockSpec((2, None, a, b), lambda g: (0, g, 0, 0))
    sd = lambda a, b: jax.ShapeDtypeStruct((2, G, a, b), F32)
    m1r, m1i, wr, wi, g2, dr, di = pl.pallas_call(
        _s5_param_kernel,
        out_shape=[sd(256, 64), sd(256, 64), sd(64, 256), sd(64, 256), sd(16, 256),
                   sd(1, 64), sd(1, 64)],
        grid=(G,),
        in_specs=[blk(1, 64), blk(1, 64), blk(64, 1), blk(64, 1), blk(1, 1),
                  blk(256, 64), blk(256, 64), blk(64, 256), blk(64, 256)],
        out_specs=[blk(256, 64), blk(256, 64), blk(64, 256), blk(64, 256), blk(16, 256),
                   blk(1, 64), blk(1, 64)],
        compiler_params=_params(("parallel",)),
        name="s5_params",
    )(lr_row, li_row, lr_col, li_col, ls, bt(b_re), bt(b_im), ct(c_re), ct(c_im))

    m1 = jnp.concatenate([m1r[0], m1r[1], m1i[0], m1i[1]], axis=-1)
    wa = jnp.concatenate([wr[0], wr[1]], axis=1)
    wb = jnp.concatenate([wi[0], wi[1]], axis=1)
    kf = g2[0].reshape(G, 16, 16, 16)
    kb = g2[1].reshape(G, 16, 16, 16)
    ribbon = jnp.concatenate([kb[:, :, :0:-1], kf[:, :, 0:1] + kb[:, :, 0:1], kf[:, :, 1:]],
                             axis=2).reshape(G, 16, 31 * 16)
    tz = jnp.stack([ribbon[:, :, 16 * (15 - s):16 * (15 - s) + 256] for s in range(16)],
                   axis=1).reshape(G, 256, 256)
    m2 = jnp.concatenate([tz, wa, wb], axis=1)
    dec_r = jnp.concatenate([dr[0], dr[1]], axis=-1)
    dec_i = jnp.concatenate([di[0], di[1]], axis=-1)
    return m1.astype(BF16), m2.astype(BF16), dec_r, dec_i


def _s5_kernel(u_ref, m1_ref, m2_ref, dr_ref, di_ref, y_ref, ea, eb, sa, sb, *, nb, nc):
    e = jnp.dot(u_ref[...], m1_ref[...], preferred_element_type=F32)
    ea[...] = e[:, 0:128]
    eb[...] = e[:, 128:256]
    ar = dr_ref[...]
    ai = di_ref[...]
    fwd = lax.broadcasted_iota(jnp.int32, (nb, 128), 1) < S5_STATE

    def scan(c, carry):
        s_re, s_im = carry
        r0 = pl.multiple_of(c * nb, nb)
        r1 = pl.multiple_of((nc - 1 - c) * nb, nb)
        sa[pl.ds(r0, nb), :] = s_re
        sb[pl.ds(r0, nb), :] = s_im
        in_re = jnp.where(fwd, ea[pl.ds(r0, nb), :], ea[pl.ds(r1, nb), :])
        in_im = jnp.where(fwd, eb[pl.ds(r0, nb), :], eb[pl.ds(r1, nb), :])
        return (ar * s_re - ai * s_im + in_re, ar * s_im + ai * s_re + in_im)

    zero = jnp.zeros((nb, 128), F32)
    lax.fori_loop(0, nc, scan, (zero, zero))

    def unflip(c, _):
        r0 = pl.multiple_of(c * nb, nb)
        r1 = pl.multiple_of((nc - 1 - c) * nb, nb)
        ea[pl.ds(r0, nb), :] = jnp.where(fwd, sa[pl.ds(r0, nb), :], sa[pl.ds(r1, nb), :])
        eb[pl.ds(r0, nb), :] = jnp.where(fwd, sb[pl.ds(r0, nb), :], sb[pl.ds(r1, nb), :])
        return 0

    lax.fori_loop(0, nc, unflip, 0)
    y = jnp.dot(u_ref[...], m2_ref[0:256, :], preferred_element_type=F32)
    y += jnp.dot(ea[...].astype(BF16), m2_ref[256:384, :], preferred_element_type=F32)
    y += jnp.dot(eb[...].astype(BF16), m2_ref[384:512, :], preferred_element_type=F32)
    y_ref[...] = y


def _s5_core(u_rows, m1, m2, dec_r, dec_i, nb, nc):
    m = nb * nc
    grp = lambda a, b: pl.BlockSpec((None, a, b), lambda g: (g, 0, 0))
    return pl.pallas_call(
        functools.partial(_s5_kernel, nb=nb, nc=nc),
        out_shape=jax.ShapeDtypeStruct((S5_GROUPS, m, S5_LANES), F32),
        grid=(S5_GROUPS,),
        in_specs=[grp(m, 256), grp(256, 256), grp(512, 256), grp(1, 128), grp(1, 128)],
        out_specs=grp(m, 256),
        scratch_shapes=[pltpu.VMEM((m, 128), F32)] * 4,
        compiler_params=_params(("parallel",)),
        name="s5_core",
    )(u_rows, m1, m2, dec_r, dec_i)


def _attn_kernel(q_ref, k_ref, v_ref, xq_ref, mk_ref, mv_ref, lq1, lk1, lq2, lk2, sg_ref,
                 yd_ref, yx_ref, vext):
    seq_len = k_ref.shape[0]

    @pl.when(pl.program_id(1) == 0)
    def _():
        for n in range(DIFF_HEADS):
            vext[n, :, 0:128] = v_ref[:, 128 * n:128 * n + 128]
            vext[n, :, 128:256] = jnp.ones((seq_len, 128), BF16)

    lam = (jnp.exp(jnp.sum(lq1[...] * lk1[...], axis=-1, keepdims=True))
           - jnp.exp(jnp.sum(lq2[...] * lk2[...], axis=-1, keepdims=True))
           + LAMBDA_INIT)
    first = lax.broadcasted_iota(jnp.int32, (1, 128), 1) < DIFF_DH
    zero = jnp.zeros((), BF16)
    for n in range(DIFF_HEADS):
        lo = 128 * n
        q = q_ref[:, lo:lo + 128]
        k = k_ref[:, lo:lo + 128]
        comps = []
        for c in range(2):
            qc = jnp.where(first if c == 0 else jnp.logical_not(first), q, zero)
            s = lax.dot_general(qc, k, (((1,), (1,)), ((), ())), preferred_element_type=F32)
            e = jnp.exp2(s - jnp.max(s, axis=-1, keepdims=True)).astype(BF16)
            pv = jnp.dot(e, vext[n], preferred_element_type=F32)
            comps.append(pv[:, 0:128] / pv[:, 128:256])
        o = comps[0] - lam * comps[1]
        o = _rms(o, sg_ref[...]) * (1.0 - LAMBDA_INIT)
        yd_ref[:, lo:lo + 128] = o.astype(BF16)

    for h in range(X_HEADS):
        lo = 128 * h
        s = lax.dot_general(xq_ref[:, lo:lo + 128], mk_ref[:, lo:lo + 128],
                            (((1,), (1,)), ((), ())), preferred_element_type=F32)
        s = s * (X_DH ** -0.5)
        e = jnp.exp(s - jnp.max(s, axis=-1, keepdims=True))
        p = e * (1.0 / jnp.sum(e, axis=-1, keepdims=True))
        yx_ref[:, lo:lo + 128] = jnp.dot(p.astype(BF16), mv_ref[:, lo:lo + 128],
                                         preferred_element_type=F32).astype(BF16)


def _attention(q, k, v, xq, mk, mv, lq1, lk1, lq2, lk2, subln, nbatch, seq_len, tq=1024):
    nq = seq_len // tq
    vec = _const_spec((1, DIFF_DH))
    qblk = pl.BlockSpec((tq, 512), lambda b, i: (b * nq + i, 0))
    kv = pl.BlockSpec((seq_len, 512), lambda b, i: (b, 0))
    mem = pl.BlockSpec((N_MEM, 512), lambda b, i: (b, 0))
    return pl.pallas_call(
        _attn_kernel,
        out_shape=[jax.ShapeDtypeStruct(q.shape, BF16)] * 2,
        grid=(nbatch, nq),
        in_specs=[qblk, kv, kv, qblk, mem, mem, vec, vec, vec, vec, _const_spec((1, DIFF_VDH))],
        out_specs=[qblk, qblk],
        scratch_shapes=[pltpu.VMEM((DIFF_HEADS, seq_len, 256), BF16)],
        compiler_params=_params(("parallel", "arbitrary")),
        name="attention",
    )(q, k, v, xq, mk, mv, lq1, lk1, lq2, lk2, subln)


def _merge_kernel(x_ref, u_ref, yr_ref, yd_ref, yx_ref,
                  gmix_ref, wg_ref, d_ref, wglu_ref, bglu_ref, wup_ref, wout_ref, o_ref, ys):
    rows = SEQ_BLK * S5_ROW
    lane_block = lax.broadcasted_iota(jnp.int32, (SEQ_BLK, 128), 1) // S5_GROUP
    for c in range(STEP_BLK // S5_ROW):
        ts = slice(c * S5_ROW, (c + 1) * S5_ROW)
        for hf in range(2):
            for col in range(4):
                outs = _block_transpose8(
                    [yr_ref[8 * col + k, c, :, hf * 128:(hf + 1) * 128] for k in range(8)],
                    lane_block)
                for m in range(8):
                    ys[c, col, pl.ds(hf * 8 + m, SEQ_BLK, stride=S5_ROW), :] = outs[m]

        x = x_ref[:, ts, :].reshape(rows, D_MODEL)
        xn = _rms(x, gmix_ref[...]).astype(BF16)

        y = (jnp.concatenate([ys[c, col] for col in range(4)], axis=-1)
             + d_ref[...] * u_ref[:, ts, :].reshape(rows, 512))
        z = jax.nn.gelu(y)
        zg = jnp.dot(z.astype(BF16), wglu_ref[...], preferred_element_type=F32) + bglu_ref[...]
        y_s5 = z * jax.nn.sigmoid(zg)

        branches = (y_s5.astype(BF16), yd_ref[:, ts, :].reshape(rows, 512),
                    yx_ref[:, ts, :].reshape(rows, 512))
        merged = None
        for n in range(3):
            g = jnp.dot(xn, wg_ref[:, n * D_MODEL:(n + 1) * D_MODEL],
                        preferred_element_type=F32)
            up = jnp.dot(branches[n], wup_ref[n], preferred_element_type=F32)
            term = jax.nn.sigmoid(g) * up
            merged = term if merged is None else merged + term
        out = x + jnp.dot(merged.astype(BF16), wout_ref[...], preferred_element_type=F32)
        o_ref[:, ts, :] = out.reshape(SEQ_BLK, S5_ROW, D_MODEL)


def _merge(x, u, y_rows, yd, yx, norm_mix, w_g, s5_d, w_glu, b_glu, w_up, w_out):
    nb, seq_len, _ = x.shape
    return pl.pallas_call(
        _merge_kernel,
        out_shape=jax.ShapeDtypeStruct((nb, seq_len, D_MODEL), F32),
        grid=(nb // SEQ_BLK, seq_len // STEP_BLK),
        in_specs=[_tok_blk(D_MODEL), _tok_blk(512), _s5_blk(), _tok_blk(512), _tok_blk(512),
                  _const_spec((1, D_MODEL)), _const_spec((D_MODEL, 3 * D_MODEL)),
                  _const_spec((1, 512)), _const_spec((512, 512)), _const_spec((1, 512)),
                  _const_spec((3, BRANCH_WIDTH, D_MODEL)), _const_spec((D_MODEL, D_MODEL))],
        out_specs=_tok_blk(D_MODEL),
        scratch_shapes=[pltpu.VMEM((STEP_BLK // S5_ROW, 4, SEQ_BLK * S5_ROW, 128), F32)],
        compiler_params=_params(("parallel", "parallel")),
        name="merge",
    )(x, u, y_rows, yd, yx, norm_mix, w_g, s5_d, w_glu, b_glu, w_up, w_out)


def _ffn_kernel(x_ref, g_ref, win_ref, wout_ref, gf_ref, o_ref):
    for hb in range(x_ref.shape[0] // FFN_ROWS):
        rs = slice(hb * FFN_ROWS, (hb + 1) * FFN_ROWS)
        x = x_ref[rs, :]
        xn = _rms(x, g_ref[...]).astype(BF16)
        gate = jnp.dot(xn, win_ref[:, 0:D_FF], preferred_element_type=F32)
        up = jnp.dot(xn, win_ref[:, D_FF:2 * D_FF], preferred_element_type=F32)
        act = (jax.nn.silu(gate) * up).astype(BF16)
        x = x + jnp.dot(act, wout_ref[...], preferred_element_type=F32)
        o_ref[rs, :] = _rms(x, gf_ref[...])


def _ffn(x1, norm_ffn, w_in, w_out, norm_final, tm=1024):
    t = x1.shape[0]
    row = lambda i: (i, 0)
    return pl.pallas_call(
        _ffn_kernel,
        out_shape=jax.ShapeDtypeStruct((t, D_MODEL), F32),
        grid=(t // tm,),
        in_specs=[pl.BlockSpec((tm, D_MODEL), row), _const_spec((1, D_MODEL)),
                  _const_spec((D_MODEL, 2 * D_FF)), _const_spec((D_FF, D_MODEL)),
                  _const_spec((1, D_MODEL))],
        out_specs=pl.BlockSpec((tm, D_MODEL), row),
        compiler_params=_params(("parallel",)),
        name="ffn",
    )(x1, norm_ffn, w_in, w_out, norm_final)


def _rope_tables(seq_len):
    half = ROPE_DIM // 2
    inv = 1.0 / (ROPE_THETA ** (jnp.arange(0, ROPE_DIM, 2, dtype=F32) / ROPE_DIM))
    ang = jnp.arange(seq_len, dtype=F32)[:, None] * inv[None, :]
    cos, sin = jnp.cos(ang), jnp.sin(ang)
    pad = jnp.zeros((seq_len, DIFF_DH - ROPE_DIM), F32)
    cos_h = jnp.concatenate([cos, cos, pad + 1.0], axis=-1)
    sa_h = jnp.concatenate([jnp.zeros_like(sin), sin, pad], axis=-1)
    sb_h = jnp.concatenate([-sin, jnp.zeros_like(sin), pad], axis=-1)
    two = lambda a: jnp.concatenate([a, a], axis=-1)
    return two(cos_h), two(sa_h), two(sb_h)


def _trunk(x, mem, w, s5m):
    nb, seq_len, _ = x.shape
    nc = seq_len // S5_ROW
    t = nb * seq_len
    cos_t, sa_t, sb_t = _rope_tables(seq_len)
    u, u_rows, q, k, v, xq = _inproj(x, w["norm_mix"], w["w_a"], cos_t, sa_t, sb_t)
    mk, mv = _memkv(mem.reshape(nb * N_MEM, D_MODEL), w["norm_mem"], w["w_mem_kv"])
    y_rows = _s5_core(u_rows.reshape(S5_GROUPS, nc * nb, S5_LANES), *s5m, nb=nb, nc=nc)
    flat = lambda a: a.reshape(t, 512)
    yd, yx = _attention(flat(q), flat(k), flat(v), flat(xq), mk, mv,
                        w["lq1"], w["lk1"], w["lq2"], w["lk2"], w["subln"], nb, seq_len)
    seq = lambda a: a.reshape(nb, seq_len, 512)
    x1 = _merge(x, u, y_rows.reshape(S5_GROUPS, nc, nb, S5_LANES), seq(yd), seq(yx),
                w["norm_mix"], w["w_g"], w["s5_d"], w["w_glu"], w["b_glu"], w["w_up"], w["w_out"])
    out = _ffn(x1.reshape(t, D_MODEL), w["norm_ffn"], w["w_ffn_in"], w["w_ffn_out"],
               w["norm_final"])
    return out.reshape(nb, seq_len, D_MODEL)


def kernel(x_prompt, x_sample, mem_prompt, mem_sample, norm_mix, norm_mem, w_in, s5_lambda_re, s5_lambda_im, s5_log_step, s5_b_re, s5_b_im, s5_c_re, s5_c_im, s5_d, s5_w_glu, s5_b_glu, diff_lambda_q1, diff_lambda_k1, diff_lambda_q2, diff_lambda_k2, diff_subln, w_mem_kv, w_up, w_out, norm_ffn, w_ffn_in, w_ffn_out, norm_final):
    l = 0
    w = dict(
        norm_mix=norm_mix[l][None, :], norm_mem=norm_mem[l][None, :],
        w_a=w_in[l][:, :A_COLS].astype(BF16), w_g=w_in[l][:, A_COLS:].astype(BF16),
        s5_d=s5_d[l][None, :], w_glu=s5_w_glu[l].astype(BF16), b_glu=s5_b_glu[l][None, :],
        lq1=diff_lambda_q1[l][None, :], lk1=diff_lambda_k1[l][None, :],
        lq2=diff_lambda_q2[l][None, :], lk2=diff_lambda_k2[l][None, :],
        subln=diff_subln[l][None, :], w_mem_kv=w_mem_kv[l].astype(BF16),
        w_up=w_up[l].astype(BF16), w_out=w_out[l].astype(BF16),
        norm_ffn=norm_ffn[l][None, :], w_ffn_in=w_ffn_in[l].astype(BF16),
        w_ffn_out=w_ffn_out[l].astype(BF16), norm_final=norm_final[None, :],
    )
    s5m = _s5_matrices(s5_lambda_re[l], s5_lambda_im[l], s5_log_step[l],
                       s5_b_re[l], s5_b_im[l], s5_c_re[l], s5_c_im[l])
    return (_trunk(x_prompt, mem_prompt, w, s5m), _trunk(x_sample, mem_sample, w, s5m))
```

```python
import functools
import math

import jax
import jax.numpy as jnp
from jax import lax
from jax.experimental import pallas as pl
from jax.experimental.pallas import tpu as pltpu

F32 = jnp.float32
BF16 = jnp.bfloat16

D_MODEL = 1024
N_MEM = 256
EPS = 1e-6
S5_WIDTH = 512
S5_GROUP = 16
S5_GROUPS = 32
S5_STATE = 64
DIFF_HEADS = 4
DIFF_DH = 64
DIFF_VDH = 128
X_HEADS = 4
X_DH = 128
BRANCH_WIDTH = 512
D_FF = 2816
ROPE_THETA = 500000.0
ROPE_DIM = 16
LAMBDA_INIT = 0.8 - 0.6 * math.exp(-0.3 * 0)
LOG2E = 1.4426950408889634
Q_SCALE = DIFF_DH ** -0.5 * LOG2E
A_COLS = 5 * 512
S5_ROW = 16
S5_LANES = S5_ROW * S5_GROUP
SEQ_BLK = 16
STEP_BLK = 64
FFN_ROWS = 256

VMEM_LIMIT = 56 * 1024 * 1024


def _const_spec(shape):
    nd = len(shape)
    return pl.BlockSpec(shape, lambda *_: (0,) * nd, pipeline_mode=pl.Buffered(1))


def _params(sem):
    return pltpu.CompilerParams(dimension_semantics=sem, vmem_limit_bytes=VMEM_LIMIT)


def _rms(x, g):
    return x * lax.rsqrt(jnp.mean(x * x, axis=-1, keepdims=True) + EPS) * g


def _block_transpose8(xs, lane_block):
    rolled = []
    for r in range(8):
        s = xs[(7 + r) % 8]
        for k in range(6, -1, -1):
            s = jnp.where(lane_block == k, xs[(k + r) % 8], s)
        rolled.append(s if r == 0 else pltpu.roll(s, 16 * r, 1))
    outs = []
    for k in range(8):
        o = rolled[(7 - k) % 8]
        for m in range(6, -1, -1):
            o = jnp.where(lane_block == m, rolled[(m - k) % 8], o)
        outs.append(o)
    return outs


def _inproj_kernel(x_ref, g_ref, w_ref, cos_ref, sa_ref, sb_ref,
                   u_ref, ur_ref, q_ref, k_ref, v_ref, xq_ref, us):
    rows = SEQ_BLK * S5_ROW
    lane_block = lax.broadcasted_iota(jnp.int32, (SEQ_BLK, 128), 1) // S5_GROUP
    blk3 = lambda a, w: a.reshape(SEQ_BLK, S5_ROW, w)
    for c in range(STEP_BLK // S5_ROW):
        ts = slice(c * S5_ROW, (c + 1) * S5_ROW)
        xn = _rms(x_ref[:, ts, :].reshape(rows, D_MODEL), g_ref[...]).astype(BF16)
        h = jnp.dot(xn, w_ref[...], preferred_element_type=F32)
        u = h[:, 0:512]
        u_ref[:, ts, :] = blk3(u, 512)
        for col in range(4):
            us[c, col] = u[:, col * 128:(col + 1) * 128]

        for hf in range(2):
            for col in range(4):
                outs = _block_transpose8(
                    [us[c, col, pl.ds(hf * 8 + m, SEQ_BLK, stride=S5_ROW), :] for m in range(8)],
                    lane_block)
                for k in range(8):
                    ur_ref[8 * col + k, c, :, hf * 128:(hf + 1) * 128] = outs[k].astype(BF16)

        tile = lambda t: jnp.broadcast_to(t[None], (SEQ_BLK, S5_ROW, 128)).reshape(rows, 128)
        cos = tile(cos_ref[ts, :])
        sa = tile(sa_ref[ts, :])
        sb = tile(sb_ref[ts, :])
        for j in range(4):
            lo = 128 * j
            tq = h[:, 512 + lo:512 + lo + 128]
            tk = h[:, 1024 + lo:1024 + lo + 128]
            rq = tq * cos + pltpu.roll(tq, 8, 1) * sa + pltpu.roll(tq, 120, 1) * sb
            rk = tk * cos + pltpu.roll(tk, 8, 1) * sa + pltpu.roll(tk, 120, 1) * sb
            q_ref[:, ts, lo:lo + 128] = blk3((rq * Q_SCALE).astype(BF16), 128)
            k_ref[:, ts, lo:lo + 128] = blk3(rk.astype(BF16), 128)
        v_ref[:, ts, :] = blk3(h[:, 1536:2048].astype(BF16), 512)
        xq_ref[:, ts, :] = blk3(h[:, 2048:2560].astype(BF16), 512)


def _tok_blk(width):
    return pl.BlockSpec((SEQ_BLK, STEP_BLK, width), lambda b, t: (b, t, 0))


def _s5_blk():
    return pl.BlockSpec((S5_GROUPS, STEP_BLK // S5_ROW, SEQ_BLK, S5_LANES),
                        lambda b, t: (0, t, b, 0))


def _inproj(x, norm_mix, w_a, cos_t, sa_t, sb_t):
    nb, seq_len, _ = x.shape
    nc = seq_len // S5_ROW
    tab = pl.BlockSpec((STEP_BLK, 128), lambda b, t: (t, 0))
    o512 = lambda dt: jax.ShapeDtypeStruct((nb, seq_len, 512), dt)
    outs = [o512(F32), jax.ShapeDtypeStruct((S5_GROUPS, nc, nb, S5_LANES), BF16)] + [o512(BF16)] * 4
    return pl.pallas_call(
        _inproj_kernel,
        out_shape=outs,
        grid=(nb // SEQ_BLK, seq_len // STEP_BLK),
        in_specs=[_tok_blk(D_MODEL), _const_spec((1, D_MODEL)), _const_spec((D_MODEL, A_COLS)),
                  tab, tab, tab],
        out_specs=[_tok_blk(512), _s5_blk()] + [_tok_blk(512)] * 4,
        scratch_shapes=[pltpu.VMEM((STEP_BLK // S5_ROW, 4, SEQ_BLK * S5_ROW, 128), F32)],
        compiler_params=_params(("parallel", "parallel")),
        name="inproj",
    )(x, norm_mix, w_a, cos_t, sa_t, sb_t)


def _memkv_kernel(m_ref, g_ref, w_ref, mk_ref, mv_ref):
    mn = _rms(m_ref[...], g_ref[...]).astype(BF16)
    h = jnp.dot(mn, w_ref[...], preferred_element_type=F32)
    mk_ref[...] = h[:, 0:512].astype(BF16)
    mv_ref[...] = h[:, 512:1024].astype(BF16)


def _memkv(mem2, norm_mem, w_kv, tm=512):
    t = mem2.shape[0]
    row = lambda i: (i, 0)
    return pl.pallas_call(
        _memkv_kernel,
        out_shape=[jax.ShapeDtypeStruct((t, 512), BF16)] * 2,
        grid=(t // tm,),
        in_specs=[pl.BlockSpec((tm, D_MODEL), row), _const_spec((1, D_MODEL)),
                  _const_spec((D_MODEL, 1024))],
        out_specs=[pl.BlockSpec((tm, 512), row)] * 2,
        compiler_params=_params(("parallel",)),
        name="memkv",
    )(mem2, norm_mem, w_kv)


def _cexp(mag_arg, ang):
    m = jnp.exp(mag_arg)
    return m * jnp.cos(ang), m * jnp.sin(ang)


def _s5_param_kernel(lr_row, li_row, lr_col, li_col, ls_ref, bt_r, bt_i, ct_r, ct_i,
                     m1r_ref, m1i_ref, wr_ref, wi_ref, g2_ref, dr_ref, di_ref):
    s_idx = (lax.broadcasted_iota(jnp.int32, (S5_LANES, 1), 0) // S5_GROUP).astype(F32)
    i_idx = (lax.broadcasted_iota(jnp.int32, (1, S5_LANES), 1) // S5_GROUP).astype(F32)
    for d in range(2):
        step = jnp.exp(ls_ref[d])
        lr = lr_row[d]
        li = li_row[d]
        ar = lr * step
        ai = li * step
        lb_r, lb_i = _cexp(ar, ai)
        den = lr * lr + li * li
        nr = lb_r - 1.0
        cf_r = (nr * lr + lb_i * li) / den
        cf_i = (lb_i * lr - nr * li) / den
        btr = bt_r[d]
        bti = bt_i[d]
        bb_r = cf_r * btr - cf_i * bti
        bb_i = cf_r * bti + cf_i * btr
        e = (15.0 - s_idx) if d == 0 else s_idx
        pw_r, pw_i = _cexp(e * ar, e * ai)
        m1r_ref[d] = pw_r * bb_r - pw_i * bb_i
        m1i_ref[d] = pw_r * bb_i + pw_i * bb_r
        dr, di = _cexp(16.0 * ar, 16.0 * ai)
        dr_ref[d] = dr
        di_ref[d] = di

        acr = lr_col[d] * step
        aci = li_col[d] * step
        ctr = ct_r[d]
        cti = ct_i[d]
        jw = (i_idx + 1.0) if d == 0 else (16.0 - i_idx)
        qr, qi = _cexp(jw * acr, jw * aci)
        wr_ref[d] = ctr * qr - cti * qi
        wi_ref[d] = -(ctr * qi + cti * qr)
        gr, gi = _cexp(i_idx * acr, i_idx * aci)
        cl_r = ctr * gr - cti * gi
        cl_i = ctr * gi + cti * gr
        g2_ref[d] = (jnp.dot(bb_r[0:16], cl_r, preferred_element_type=F32,
                             precision=lax.Precision.HIGHEST)
                     - jnp.dot(bb_i[0:16], cl_i, preferred_element_type=F32,
                               precision=lax.Precision.HIGHEST))


def _s5_matrices(lam_re, lam_im, log_step, b_re, b_im, c_re, c_im):
    G = S5_GROUPS
    lr_row = lam_re.reshape(2, G, 1, S5_STATE)
    li_row = lam_im.reshape(2, G, 1, S5_STATE)
    lr_col = lam_re.reshape(2, G, S5_STATE, 1)
    li_col = lam_im.reshape(2, G, S5_STATE, 1)
    ls = log_step.reshape(2, G, 1, 1)
    bt = lambda b: jnp.tile(jnp.swapaxes(b, -1, -2), (1, 1, S5_ROW, 1))
    ct = lambda c: jnp.tile(jnp.swapaxes(c, -1, -2), (1, 1, 1, S5_ROW))
    blk = lambda a, b: pl.BlockSpec((2, None, a, b), lambda g: (0, g, 0, 0))
    sd = lambda a, b: jax.ShapeDtypeStruct((2, G, a, b), F32)
    m1r, m1i, wr, wi, g2, dr, di = pl.pallas_call(
        _s5_param_kernel,
        out_shape=[sd(256, 64), sd(256, 64), sd(64, 256), sd(64, 256), sd(16, 256),
                   sd(1, 64), sd(1, 64)],
        grid=(G,),
        in_specs=[blk(1, 64), blk(1, 64), blk(64, 1), blk(64, 1), blk(1, 1),
                  blk(256, 64), blk(256, 64), blk(64, 256), blk(64, 256)],
        out_specs=[blk(256, 64), blk(256, 64), blk(64, 256), blk(64, 256), blk(16, 256),
                   blk(1, 64), blk(1, 64)],
        compiler_params=_params(("parallel",)),
        name="s5_params",
    )(lr_row, li_row, lr_col, li_col, ls, bt(b_re), bt(b_im), ct(c_re), ct(c_im))

    m1 = jnp.concatenate([m1r[0], m1r[1], m1i[0], m1i[1]], axis=-1)
    wa = jnp.concatenate([wr[0], wr[1]], axis=1)
    wb = jnp.concatenate([wi[0], wi[1]], axis=1)
    kf = g2[0].reshape(G, 16, 16, 16)
    kb = g2[1].reshape(G, 16, 16, 16)
    ribbon = jnp.concatenate([kb[:, :, :0:-1], kf[:, :, 0:1] + kb[:, :, 0:1], kf[:, :, 1:]],
                             axis=2).reshape(G, 16, 31 * 16)
    tz = jnp.stack([ribbon[:, :, 16 * (15 - s):16 * (15 - s) + 256] for s in range(16)],
                   axis=1).reshape(G, 256, 256)
    m2 = jnp.concatenate([tz, wa, wb], axis=1)
    dec_r = jnp.concatenate([dr[0], dr[1]], axis=-1)
    dec_i = jnp.concatenate([di[0], di[1]], axis=-1)
    return m1.astype(BF16), m2.astype(BF16), dec_r, dec_i


def _s5_kernel(u_ref, m1_ref, m2_ref, dr_ref, di_ref, y_ref, ea, eb, st, *, nb, nc):
    e = jnp.dot(u_ref[...], m1_ref[...], preferred_element_type=F32)
    ea[...] = e[:, 0:128]
    eb[...] = e[:, 128:256]
    ar = dr_ref[...]
    ai = di_ref[...]
    fwd = lax.broadcasted_iota(jnp.int32, (nb, 128), 1) < S5_STATE
    h = S5_STATE

    def scan(c, carry):
        s_re, s_im = carry
        r0 = c * nb
        r1 = (nc - 1 - c) * nb
        st[pl.ds(r0, nb), 0:h] = s_re[:, 0:h].astype(BF16)
        st[pl.ds(r1, nb), h:2 * h] = s_re[:, h:2 * h].astype(BF16)
        st[pl.ds(r0, nb), 2 * h:3 * h] = s_im[:, 0:h].astype(BF16)
        st[pl.ds(r1, nb), 3 * h:4 * h] = s_im[:, h:2 * h].astype(BF16)
        in_re = jnp.where(fwd, ea[pl.ds(r0, nb), :], ea[pl.ds(r1, nb), :])
        in_im = jnp.where(fwd, eb[pl.ds(r0, nb), :], eb[pl.ds(r1, nb), :])
        return (ar * s_re - ai * s_im + in_re, ar * s_im + ai * s_re + in_im)

    y = jnp.dot(u_ref[...], m2_ref[0:256, :], preferred_element_type=F32)
    carry = (jnp.zeros((nb, 128), F32), jnp.zeros((nb, 128), F32))
    for c in range(nc):
        carry = scan(c, carry)
    y += jnp.dot(st[...], m2_ref[256:512, :], preferred_element_type=F32)
    y_ref[...] = y


def _s5_core(u_rows, m1, m2, dec_r, dec_i, nb, nc):
    m = nb * nc
    grp = lambda a, b: pl.BlockSpec((None, a, b), lambda g: (g, 0, 0))
    return pl.pallas_call(
        functools.partial(_s5_kernel, nb=nb, nc=nc),
        out_shape=jax.ShapeDtypeStruct((S5_GROUPS, m, S5_LANES), F32),
        grid=(S5_GROUPS,),
        in_specs=[grp(m, 256), grp(256, 256), grp(512, 256), grp(1, 128), grp(1, 128)],
        out_specs=grp(m, 256),
        scratch_shapes=[pltpu.VMEM((m, 128), F32), pltpu.VMEM((m, 128), F32),
                        pltpu.VMEM((m, S5_LANES), BF16)],
        compiler_params=_params(("parallel",)),
        name="s5_core",
    )(u_rows, m1, m2, dec_r, dec_i)


def _attn_kernel(q_ref, k_ref, v_ref, xq_ref, mk_ref, mv_ref, lq1, lk1, lq2, lk2, sg_ref,
                 yd_ref, yx_ref, vext):
    seq_len = k_ref.shape[0]

    @pl.when(pl.program_id(1) == 0)
    def _():
        for n in range(DIFF_HEADS):
            vext[n, :, 0:128] = v_ref[:, 128 * n:128 * n + 128]
            vext[n, :, 128:256] = jnp.ones((seq_len, 128), BF16)

    lam = (jnp.exp(jnp.sum(lq1[...] * lk1[...], axis=-1, keepdims=True))
           - jnp.exp(jnp.sum(lq2[...] * lk2[...], axis=-1, keepdims=True))
           + LAMBDA_INIT)
    first = lax.broadcasted_iota(jnp.int32, (1, 128), 1) < DIFF_DH
    zero = jnp.zeros((), BF16)
    for n in range(DIFF_HEADS):
        lo = 128 * n
        q = q_ref[:, lo:lo + 128]
        k = k_ref[:, lo:lo + 128]
        comps = []
        for c in range(2):
            qc = jnp.where(first if c == 0 else jnp.logical_not(first), q, zero)
            s = lax.dot_general(qc, k, (((1,), (1,)), ((), ())), preferred_element_type=F32)
            e = jnp.exp2(s - jnp.max(s, axis=-1, keepdims=True)).astype(BF16)
            pv = jnp.dot(e, vext[n], preferred_element_type=F32)
            comps.append(pv[:, 0:128] / pv[:, 128:256])
        o = comps[0] - lam * comps[1]
        o = _rms(o, sg_ref[...]) * (1.0 - LAMBDA_INIT)
        yd_ref[:, lo:lo + 128] = o.astype(BF16)

    for h in range(X_HEADS):
        lo = 128 * h
        s = lax.dot_general(xq_ref[:, lo:lo + 128], mk_ref[:, lo:lo + 128],
                            (((1,), (1,)), ((), ())), preferred_element_type=F32)
        s = s * (X_DH ** -0.5)
        e = jnp.exp(s - jnp.max(s, axis=-1, keepdims=True))
        p = e * (1.0 / jnp.sum(e, axis=-1, keepdims=True))
        yx_ref[:, lo:lo + 128] = jnp.dot(p.astype(BF16), mv_ref[:, lo:lo + 128],
                                         preferred_element_type=F32).astype(BF16)


def _attention(q, k, v, xq, mk, mv, lq1, lk1, lq2, lk2, subln, nbatch, seq_len, tq=1024):
    nq = seq_len // tq
    vec = _const_spec((1, DIFF_DH))
    qblk = pl.BlockSpec((tq, 512), lambda b, i: (b * nq + i, 0))
    kv = pl.BlockSpec((seq_len, 512), lambda b, i: (b, 0))
    mem = pl.BlockSpec((N_MEM, 512), lambda b, i: (b, 0))
    return pl.pallas_call(
        _attn_kernel,
        out_shape=[jax.ShapeDtypeStruct(q.shape, BF16)] * 2,
        grid=(nbatch, nq),
        in_specs=[qblk, kv, kv, qblk, mem, mem, vec, vec, vec, vec, _const_spec((1, DIFF_VDH))],
        out_specs=[qblk, qblk],
        scratch_shapes=[pltpu.VMEM((DIFF_HEADS, seq_len, 256), BF16)],
        compiler_params=_params(("parallel", "arbitrary")),
        name="attention",
    )(q, k, v, xq, mk, mv, lq1, lk1, lq2, lk2, subln)


def _merge_kernel(x_ref, u_ref, yr_ref, yd_ref, yx_ref,
                  gmix_ref, wg_ref, d_ref, wglu_ref, bglu_ref, wup_ref, wout_ref, o_ref, ys):
    rows = SEQ_BLK * S5_ROW
    lane_block = lax.broadcasted_iota(jnp.int32, (SEQ_BLK, 128), 1) // S5_GROUP
    for c in range(STEP_BLK // S5_ROW):
        ts = slice(c * S5_ROW, (c + 1) * S5_ROW)
        for hf in range(2):
            for col in range(4):
                outs = _block_transpose8(
                    [yr_ref[8 * col + k, c, :, hf * 128:(hf + 1) * 128] for k in range(8)],
                    lane_block)
                for m in range(8):
                    ys[c, col, pl.ds(hf * 8 + m, SEQ_BLK, stride=S5_ROW), :] = outs[m]

        x = x_ref[:, ts, :].reshape(rows, D_MODEL)
        xn = _rms(x, gmix_ref[...]).astype(BF16)

        y = (jnp.concatenate([ys[c, col] for col in range(4)], axis=-1)
             + d_ref[...] * u_ref[:, ts, :].reshape(rows, 512))
        z = jax.nn.gelu(y)
        zg = jnp.dot(z.astype(BF16), wglu_ref[...], preferred_element_type=F32) + bglu_ref[...]
        y_s5 = z * jax.nn.sigmoid(zg)

        branches = (y_s5.astype(BF16), yd_ref[:, ts, :].reshape(rows, 512),
                    yx_ref[:, ts, :].reshape(rows, 512))
        merged = None
        for n in range(3):
            g = jnp.dot(xn, wg_ref[:, n * D_MODEL:(n + 1) * D_MODEL],
                        preferred_element_type=F32)
            up = jnp.dot(branches[n], wup_ref[n], preferred_element_type=F32)
            term = jax.nn.sigmoid(g) * up
            merged = term if merged is None else merged + term
        out = x + jnp.dot(merged.astype(BF16), wout_ref[...], preferred_element_type=F32)
        o_ref[:, ts, :] = out.reshape(SEQ_BLK, S5_ROW, D_MODEL)


def _merge(x, u, y_rows, yd, yx, norm_mix, w_g, s5_d, w_glu, b_glu, w_up, w_out):
    nb, seq_len, _ = x.shape
    return pl.pallas_call(
        _merge_kernel,
        out_shape=jax.ShapeDtypeStruct((nb, seq_len, D_MODEL), F32),
        grid=(nb // SEQ_BLK, seq_len // STEP_BLK),
        in_specs=[_tok_blk(D_MODEL), _tok_blk(512), _s5_blk(), _tok_blk(512), _tok_blk(512),
                  _const_spec((1, D_MODEL)), _const_spec((D_MODEL, 3 * D_MODEL)),
                  _const_spec((1, 512)), _const_spec((512, 512)), _const_spec((1, 512)),
                  _const_spec((3, BRANCH_WIDTH, D_MODEL)), _const_spec((D_MODEL, D_MODEL))],
        out_specs=_tok_blk(D_MODEL),
        scratch_shapes=[pltpu.VMEM((STEP_BLK // S5_ROW, 4, SEQ_BLK * S5_ROW, 128), F32)],
        compiler_params=_params(("parallel", "parallel")),
        name="merge",
    )(x, u, y_rows, yd, yx, norm_mix, w_g, s5_d, w_glu, b_glu, w_up, w_out)


def _ffn_kernel(x_ref, g_ref, win_ref, wout_ref, gf_ref, o_ref):
    for hb in range(x_ref.shape[0] // FFN_ROWS):
        rs = slice(hb * FFN_ROWS, (hb + 1) * FFN_ROWS)
        x = x_ref[rs, :]
        xn = _rms(x, g_ref[...]).astype(BF16)
        gate = jnp.dot(xn, win_ref[:, 0:D_FF], preferred_element_type=F32)
        up = jnp.dot(xn, win_ref[:, D_FF:2 * D_FF], preferred_element_type=F32)
        act = (jax.nn.silu(gate) * up).astype(BF16)
        x = x + jnp.dot(act, wout_ref[...], preferred_element_type=F32)
        o_ref[rs, :] = _rms(x, gf_ref[...])


def _ffn(x1, norm_ffn, w_in, w_out, norm_final, tm=1024):
    t = x1.shape[0]
    row = lambda i: (i, 0)
    return pl.pallas_call(
        _ffn_kernel,
        out_shape=jax.ShapeDtypeStruct((t, D_MODEL), F32),
        grid=(t // tm,),
        in_specs=[pl.BlockSpec((tm, D_MODEL), row), _const_spec((1, D_MODEL)),
                  _const_spec((D_MODEL, 2 * D_FF)), _const_spec((D_FF, D_MODEL)),
                  _const_spec((1, D_MODEL))],
        out_specs=pl.BlockSpec((tm, D_MODEL), row),
        compiler_params=_params(("parallel",)),
        name="ffn",
    )(x1, norm_ffn, w_in, w_out, norm_final)


def _rope_tables(seq_len):
    half = ROPE_DIM // 2
    inv = 1.0 / (ROPE_THETA ** (jnp.arange(0, ROPE_DIM, 2, dtype=F32) / ROPE_DIM))
    ang = jnp.arange(seq_len, dtype=F32)[:, None] * inv[None, :]
    cos, sin = jnp.cos(ang), jnp.sin(ang)
    pad = jnp.zeros((seq_len, DIFF_DH - ROPE_DIM), F32)
    cos_h = jnp.concatenate([cos, cos, pad + 1.0], axis=-1)
    sa_h = jnp.concatenate([jnp.zeros_like(sin), sin, pad], axis=-1)
    sb_h = jnp.concatenate([-sin, jnp.zeros_like(sin), pad], axis=-1)
    two = lambda a: jnp.concatenate([a, a], axis=-1)
    return two(cos_h), two(sa_h), two(sb_h)


def _trunk(x, mem, w, s5m):
    nb, seq_len, _ = x.shape
    nc = seq_len // S5_ROW
    t = nb * seq_len
    cos_t, sa_t, sb_t = _rope_tables(seq_len)
    u, u_rows, q, k, v, xq = _inproj(x, w["norm_mix"], w["w_a"], cos_t, sa_t, sb_t)
    mk, mv = _memkv(mem.reshape(nb * N_MEM, D_MODEL), w["norm_mem"], w["w_mem_kv"])
    y_rows = _s5_core(u_rows.reshape(S5_GROUPS, nc * nb, S5_LANES), *s5m, nb=nb, nc=nc)
    flat = lambda a: a.reshape(t, 512)
    yd, yx = _attention(flat(q), flat(k), flat(v), flat(xq), mk, mv,
                        w["lq1"], w["lk1"], w["lq2"], w["lk2"], w["subln"], nb, seq_len)
    seq = lambda a: a.reshape(nb, seq_len, 512)
    x1 = _merge(x, u, y_rows.reshape(S5_GROUPS, nc, nb, S5_LANES), seq(yd), seq(yx),
                w["norm_mix"], w["w_g"], w["s5_d"], w["w_glu"], w["b_glu"], w["w_up"], w["w_out"])
    out = _ffn(x1.reshape(t, D_MODEL), w["norm_ffn"], w["w_ffn_in"], w["w_ffn_out"],
               w["norm_final"])
    return out.reshape(nb, seq_len, D_MODEL)


def kernel(x_prompt, x_sample, mem_prompt, mem_sample, norm_mix, norm_mem, w_in, s5_lambda_re, s5_lambda_im, s5_log_step, s5_b_re, s5_b_im, s5_c_re, s5_c_im, s5_d, s5_w_glu, s5_b_glu, diff_lambda_q1, diff_lambda_k1, diff_lambda_q2, diff_lambda_k2, diff_subln, w_mem_kv, w_up, w_out, norm_ffn, w_ffn_in, w_ffn_out, norm_final):
    l = 0
    w = dict(
        norm_mix=norm_mix[l][None, :], norm_mem=norm_mem[l][None, :],
        w_a=w_in[l][:, :A_COLS].astype(BF16), w_g=w_in[l][:, A_COLS:].astype(BF16),
        s5_d=s5_d[l][None, :], w_glu=s5_w_glu[l].astype(BF16), b_glu=s5_b_glu[l][None, :],
        lq1=diff_lambda_q1[l][None, :], lk1=diff_lambda_k1[l][None, :],
        lq2=diff_lambda_q2[l][None, :], lk2=diff_lambda_k2[l][None, :],
        subln=diff_subln[l][None, :], w_mem_kv=w_mem_kv[l].astype(BF16),
        w_up=w_up[l].astype(BF16), w_out=w_out[l].astype(BF16),
        norm_ffn=norm_ffn[l][None, :], w_ffn_in=w_ffn_in[l].astype(BF16),
        w_ffn_out=w_ffn_out[l].astype(BF16), norm_final=norm_final[None, :],
    )
    s5m = _s5_matrices(s5_lambda_re[l], s5_lambda_im[l], s5_log_step[l],
                       s5_b_re[l], s5_b_im[l], s5_c_re[l], s5_c_im[l])
    return (_trunk(x_prompt, mem_prompt, w, s5m), _trunk(x_sample, mem_sample, w, s5m))
```

```python
import functools
import math

import jax
import jax.numpy as jnp
from jax import lax
from jax.experimental import pallas as pl
from jax.experimental.pallas import tpu as pltpu

F32 = jnp.float32
BF16 = jnp.bfloat16

D_MODEL = 1024
N_MEM = 256
EPS = 1e-6
S5_WIDTH = 512
S5_GROUP = 16
S5_GROUPS = 32
S5_STATE = 64
DIFF_HEADS = 4
DIFF_DH = 64
DIFF_VDH = 128
X_HEADS = 4
X_DH = 128
BRANCH_WIDTH = 512
D_FF = 2816
ROPE_THETA = 500000.0
ROPE_DIM = 16
LAMBDA_INIT = 0.8 - 0.6 * math.exp(-0.3 * 0)
LOG2E = 1.4426950408889634
Q_SCALE = DIFF_DH ** -0.5 * LOG2E
A_COLS = 5 * 512
S5_ROW = 16
S5_LANES = S5_ROW * S5_GROUP
SEQ_BLK = 16
STEP_BLK = 64
FFN_ROWS = 256

VMEM_LIMIT = 56 * 1024 * 1024


def _const_spec(shape):
    nd = len(shape)
    return pl.BlockSpec(shape, lambda *_: (0,) * nd, pipeline_mode=pl.Buffered(1))


def _params(sem):
    return pltpu.CompilerParams(dimension_semantics=sem, vmem_limit_bytes=VMEM_LIMIT)


def _rms(x, g):
    return x * lax.rsqrt(jnp.mean(x * x, axis=-1, keepdims=True) + EPS) * g


def _block_transpose8(xs, lane_block):
    rolled = []
    for r in range(8):
        s = xs[(7 + r) % 8]
        for k in range(6, -1, -1):
            s = jnp.where(lane_block == k, xs[(k + r) % 8], s)
        rolled.append(s if r == 0 else pltpu.roll(s, 16 * r, 1))
    outs = []
    for k in range(8):
        o = rolled[(7 - k) % 8]
        for m in range(6, -1, -1):
            o = jnp.where(lane_block == m, rolled[(m - k) % 8], o)
        outs.append(o)
    return outs


def _inproj_kernel(x_ref, g_ref, w_ref, cos_ref, sa_ref, sb_ref,
                   u_ref, ur_ref, q_ref, k_ref, v_ref, xq_ref, us):
    rows = SEQ_BLK * S5_ROW
    lane_block = lax.broadcasted_iota(jnp.int32, (SEQ_BLK, 128), 1) // S5_GROUP
    blk3 = lambda a, w: a.reshape(SEQ_BLK, S5_ROW, w)
    for c in range(STEP_BLK // S5_ROW):
        ts = slice(c * S5_ROW, (c + 1) * S5_ROW)
        xn = _rms(x_ref[:, ts, :].reshape(rows, D_MODEL), g_ref[...]).astype(BF16)
        h = jnp.dot(xn, w_ref[...], preferred_element_type=F32)
        u = h[:, 0:512]
        u_ref[:, ts, :] = blk3(u, 512)
        for col in range(4):
            us[c, col] = u[:, col * 128:(col + 1) * 128]

        for hf in range(2):
            for col in range(4):
                outs = _block_transpose8(
                    [us[c, col, pl.ds(hf * 8 + m, SEQ_BLK, stride=S5_ROW), :] for m in range(8)],
                    lane_block)
                for k in range(8):
                    ur_ref[8 * col + k, c, :, hf * 128:(hf + 1) * 128] = outs[k].astype(BF16)

        tile = lambda t: jnp.broadcast_to(t[None], (SEQ_BLK, S5_ROW, 128)).reshape(rows, 128)
        cos = tile(cos_ref[ts, :])
        sa = tile(sa_ref[ts, :])
        sb = tile(sb_ref[ts, :])
        for j in range(4):
            lo = 128 * j
            tq = h[:, 512 + lo:512 + lo + 128]
            tk = h[:, 1024 + lo:1024 + lo + 128]
            rq = tq * cos + pltpu.roll(tq, 8, 1) * sa + pltpu.roll(tq, 120, 1) * sb
            rk = tk * cos + pltpu.roll(tk, 8, 1) * sa + pltpu.roll(tk, 120, 1) * sb
            q_ref[:, ts, lo:lo + 128] = blk3((rq * Q_SCALE).astype(BF16), 128)
            k_ref[:, ts, lo:lo + 128] = blk3(rk.astype(BF16), 128)
        v_ref[:, ts, :] = blk3(h[:, 1536:2048].astype(BF16), 512)
        xq_ref[:, ts, :] = blk3(h[:, 2048:2560].astype(BF16), 512)


def _tok_blk(width):
    return pl.BlockSpec((SEQ_BLK, STEP_BLK, width), lambda b, t: (b, t, 0))


def _s5_blk():
    return pl.BlockSpec((S5_GROUPS, STEP_BLK // S5_ROW, SEQ_BLK, S5_LANES),
                        lambda b, t: (0, t, b, 0))


def _inproj(x, norm_mix, w_a, cos_t, sa_t, sb_t):
    nb, seq_len, _ = x.shape
    nc = seq_len // S5_ROW
    tab = pl.BlockSpec((STEP_BLK, 128), lambda b, t: (t, 0))
    o512 = lambda dt: jax.ShapeDtypeStruct((nb, seq_len, 512), dt)
    outs = [o512(F32), jax.ShapeDtypeStruct((S5_GROUPS, nc, nb, S5_LANES), BF16)] + [o512(BF16)] * 4
    return pl.pallas_call(
        _inproj_kernel,
        out_shape=outs,
        grid=(nb // SEQ_BLK, seq_len // STEP_BLK),
        in_specs=[_tok_blk(D_MODEL), _const_spec((1, D_MODEL)), _const_spec((D_MODEL, A_COLS)),
                  tab, tab, tab],
        out_specs=[_tok_blk(512), _s5_blk()] + [_tok_blk(512)] * 4,
        scratch_shapes=[pltpu.VMEM((STEP_BLK // S5_ROW, 4, SEQ_BLK * S5_ROW, 128), F32)],
        compiler_params=_params(("parallel", "parallel")),
        name="inproj",
    )(x, norm_mix, w_a, cos_t, sa_t, sb_t)


def _memkv_kernel(m_ref, g_ref, w_ref, mk_ref, mv_ref):
    mn = _rms(m_ref[...], g_ref[...]).astype(BF16)
    h = jnp.dot(mn, w_ref[...], preferred_element_type=F32)
    mk_ref[...] = h[:, 0:512].astype(BF16)
    mv_ref[...] = h[:, 512:1024].astype(BF16)


def _memkv(mem2, norm_mem, w_kv, tm=512):
    t = mem2.shape[0]
    row = lambda i: (i, 0)
    return pl.pallas_call(
        _memkv_kernel,
        out_shape=[jax.ShapeDtypeStruct((t, 512), BF16)] * 2,
        grid=(t // tm,),
        in_specs=[pl.BlockSpec((tm, D_MODEL), row), _const_spec((1, D_MODEL)),
                  _const_spec((D_MODEL, 1024))],
        out_specs=[pl.BlockSpec((tm, 512), row)] * 2,
        compiler_params=_params(("parallel",)),
        name="memkv",
    )(mem2, norm_mem, w_kv)


def _cexp(mag_arg, ang):
    m = jnp.exp(mag_arg)
    return m * jnp.cos(ang), m * jnp.sin(ang)


def _s5_param_kernel(lr_row, li_row, lr_col, li_col, ls_ref, bt_r, bt_i, ct_r, ct_i,
                     m1r_ref, m1i_ref, wr_ref, wi_ref, g2_ref, dr_ref, di_ref):
    hi = lax.Precision.HIGHEST
    s_col = lax.broadcasted_iota(jnp.int32, (S5_ROW, 1), 0).astype(F32)
    j_lane = jnp.minimum(lax.broadcasted_iota(jnp.int32, (1, 128), 1), S5_ROW).astype(F32)
    jj = lax.broadcasted_iota(jnp.int32, (128, S5_LANES), 0)
    ii = lax.broadcasted_iota(jnp.int32, (128, S5_LANES), 1) // S5_GROUP
    rows3 = lambda t: t[:, None, :]
    for d in range(2):
        step = jnp.exp(ls_ref[d])
        lr = lr_row[d]
        li = li_row[d]
        ar = lr * step
        ai = li * step
        lb_r, lb_i = _cexp(ar, ai)
        den = lr * lr + li * li
        nr = lb_r - 1.0
        cf_r = (nr * lr + lb_i * li) / den
        cf_i = (lb_i * lr - nr * li) / den
        btr = bt_r[d]
        bti = bt_i[d]
        bb_r = cf_r * btr - cf_i * bti
        bb_i = cf_r * bti + cf_i * btr
        e = (15.0 - s_col) if d == 0 else s_col
        pw_r, pw_i = _cexp(e * ar, e * ai)
        m1r_ref[d] = (rows3(pw_r) * bb_r[None] - rows3(pw_i) * bb_i[None]).reshape(S5_LANES, 64)
        m1i_ref[d] = (rows3(pw_r) * bb_i[None] + rows3(pw_i) * bb_r[None]).reshape(S5_LANES, 64)
        dr, di = _cexp(16.0 * ar, 16.0 * ai)
        dr_ref[d] = dr
        di_ref[d] = di

        acr = lr_col[d] * step
        aci = li_col[d] * step
        ctr = ct_r[d]
        cti = ct_i[d]
        tb_r, tb_i = _cexp(j_lane * acr, j_lane * aci)
        spread = lambda t, sel: jnp.dot(t, jnp.where(jj == sel, 1.0, 0.0).astype(F32),
                                        preferred_element_type=F32, precision=hi)
        jw = (ii + 1) if d == 0 else (S5_ROW - ii)
        qr = spread(tb_r, jw)
        qi = spread(tb_i, jw)
        wr_ref[d] = ctr * qr - cti * qi
        wi_ref[d] = -(ctr * qi + cti * qr)
        gr = spread(tb_r, ii)
        gi = spread(tb_i, ii)
        cl_r = ctr * gr - cti * gi
        cl_i = ctr * gi + cti * gr
        g2_ref[d] = (jnp.dot(bb_r, cl_r, preferred_element_type=F32, precision=hi)
                     - jnp.dot(bb_i, cl_i, preferred_element_type=F32, precision=hi))


def _s5_matrices(lam_re, lam_im, log_step, b_re, b_im, c_re, c_im):
    G = S5_GROUPS
    lr_row = lam_re.reshape(2, G, 1, S5_STATE)
    li_row = lam_im.reshape(2, G, 1, S5_STATE)
    lr_col = lam_re.reshape(2, G, S5_STATE, 1)
    li_col = lam_im.reshape(2, G, S5_STATE, 1)
    ls = log_step.reshape(2, G, 1, 1)
    bt = lambda b: jnp.swapaxes(b, -1, -2)
    ct = lambda c: jnp.tile(jnp.swapaxes(c, -1, -2), (1, 1, 1, S5_ROW))
    blk = lambda a, b: pl.BlockSpec((2, None, a, b), lambda g: (0, g, 0, 0))
    sd = lambda a, b: jax.ShapeDtypeStruct((2, G, a, b), F32)
    m1r, m1i, wr, wi, g2, dr, di = pl.pallas_call(
        _s5_param_kernel,
        out_shape=[sd(256, 64), sd(256, 64), sd(64, 256), sd(64, 256), sd(16, 256),
                   sd(1, 64), sd(1, 64)],
        grid=(G,),
        in_specs=[blk(1, 64), blk(1, 64), blk(64, 1), blk(64, 1), blk(1, 1),
                  blk(16, 64), blk(16, 64), blk(64, 256), blk(64, 256)],
        out_specs=[blk(256, 64), blk(256, 64), blk(64, 256), blk(64, 256), blk(16, 256),
                   blk(1, 64), blk(1, 64)],
        compiler_params=_params(("parallel",)),
        name="s5_params",
    )(lr_row, li_row, lr_col, li_col, ls, bt(b_re), bt(b_im), ct(c_re), ct(c_im))

    m1 = jnp.concatenate([m1r[0], m1r[1], m1i[0], m1i[1]], axis=-1)
    wa = jnp.concatenate([wr[0], wr[1]], axis=1)
    wb = jnp.concatenate([wi[0], wi[1]], axis=1)
    kf = g2[0].reshape(G, 16, 16, 16)
    kb = g2[1].reshape(G, 16, 16, 16)
    ribbon = jnp.concatenate([kb[:, :, :0:-1], kf[:, :, 0:1] + kb[:, :, 0:1], kf[:, :, 1:]],
                             axis=2).reshape(G, 16, 31 * 16)
    tz = jnp.stack([ribbon[:, :, 16 * (15 - s):16 * (15 - s) + 256] for s in range(16)],
                   axis=1).reshape(G, 256, 256)
    m2 = jnp.concatenate([tz, wa, wb], axis=1)
    dec_r = jnp.concatenate([dr[0], dr[1]], axis=-1)
    dec_i = jnp.concatenate([di[0], di[1]], axis=-1)
    return m1.astype(BF16), m2.astype(BF16), dec_r, dec_i


def _s5_kernel(u_ref, m1_ref, m2_ref, dr_ref, di_ref, y_ref, ea, eb, st, *, nb, nc):
    e = jnp.dot(u_ref[...], m1_ref[...], preferred_element_type=F32)
    ea[...] = e[:, 0:128]
    eb[...] = e[:, 128:256]
    ar = dr_ref[...]
    ai = di_ref[...]
    fwd = lax.broadcasted_iota(jnp.int32, (nb, 128), 1) < S5_STATE
    h = S5_STATE

    def scan(c, carry):
        s_re, s_im = carry
        r0 = c * nb
        r1 = (nc - 1 - c) * nb
        st[pl.ds(r0, nb), 0:h] = s_re[:, 0:h].astype(BF16)
        st[pl.ds(r1, nb), h:2 * h] = s_re[:, h:2 * h].astype(BF16)
        st[pl.ds(r0, nb), 2 * h:3 * h] = s_im[:, 0:h].astype(BF16)
        st[pl.ds(r1, nb), 3 * h:4 * h] = s_im[:, h:2 * h].astype(BF16)
        in_re = jnp.where(fwd, ea[pl.ds(r0, nb), :], ea[pl.ds(r1, nb), :])
        in_im = jnp.where(fwd, eb[pl.ds(r0, nb), :], eb[pl.ds(r1, nb), :])
        return (ar * s_re - ai * s_im + in_re, ar * s_im + ai * s_re + in_im)

    y = jnp.dot(u_ref[...], m2_ref[0:256, :], preferred_element_type=F32)
    carry = (jnp.zeros((nb, 128), F32), jnp.zeros((nb, 128), F32))
    for c in range(nc):
        carry = scan(c, carry)
    y += jnp.dot(st[...], m2_ref[256:512, :], preferred_element_type=F32)
    y_ref[...] = y


def _s5_core(u_rows, m1, m2, dec_r, dec_i, nb, nc):
    m = nb * nc
    grp = lambda a, b: pl.BlockSpec((None, a, b), lambda g: (g, 0, 0))
    return pl.pallas_call(
        functools.partial(_s5_kernel, nb=nb, nc=nc),
        out_shape=jax.ShapeDtypeStruct((S5_GROUPS, m, S5_LANES), F32),
        grid=(S5_GROUPS,),
        in_specs=[grp(m, 256), grp(256, 256), grp(512, 256), grp(1, 128), grp(1, 128)],
        out_specs=grp(m, 256),
        scratch_shapes=[pltpu.VMEM((m, 128), F32), pltpu.VMEM((m, 128), F32),
                        pltpu.VMEM((m, S5_LANES), BF16)],
        compiler_params=_params(("parallel",)),
        name="s5_core",
    )(u_rows, m1, m2, dec_r, dec_i)


def _attn_kernel(q_ref, k_ref, v_ref, xq_ref, mk_ref, mv_ref, lq1, lk1, lq2, lk2, sg_ref,
                 yd_ref, yx_ref, vext):
    seq_len = k_ref.shape[0]

    @pl.when(pl.program_id(1) == 0)
    def _():
        for n in range(DIFF_HEADS):
            vext[n, :, 0:128] = v_ref[:, 128 * n:128 * n + 128]
            vext[n, :, 128:256] = jnp.ones((seq_len, 128), BF16)

    lam = (jnp.exp(jnp.sum(lq1[...] * lk1[...], axis=-1, keepdims=True))
           - jnp.exp(jnp.sum(lq2[...] * lk2[...], axis=-1, keepdims=True))
           + LAMBDA_INIT)
    first = lax.broadcasted_iota(jnp.int32, (1, 128), 1) < DIFF_DH
    zero = jnp.zeros((), BF16)
    for n in range(DIFF_HEADS):
        lo = 128 * n
        q = q_ref[:, lo:lo + 128]
        k = k_ref[:, lo:lo + 128]
        comps = []
        for c in range(2):
            qc = jnp.where(first if c == 0 else jnp.logical_not(first), q, zero)
            s = lax.dot_general(qc, k, (((1,), (1,)), ((), ())), preferred_element_type=F32)
            e = jnp.exp2(s - jnp.max(s, axis=-1, keepdims=True)).astype(BF16)
            pv = jnp.dot(e, vext[n], preferred_element_type=F32)
            comps.append(pv[:, 0:128] / pv[:, 128:256])
        o = comps[0] - lam * comps[1]
        o = _rms(o, sg_ref[...]) * (1.0 - LAMBDA_INIT)
        yd_ref[:, lo:lo + 128] = o.astype(BF16)

    for h in range(X_HEADS):
        lo = 128 * h
        s = lax.dot_general(xq_ref[:, lo:lo + 128], mk_ref[:, lo:lo + 128],
                            (((1,), (1,)), ((), ())), preferred_element_type=F32)
        s = s * (X_DH ** -0.5)
        e = jnp.exp(s - jnp.max(s, axis=-1, keepdims=True))
        p = e * (1.0 / jnp.sum(e, axis=-1, keepdims=True))
        yx_ref[:, lo:lo + 128] = jnp.dot(p.astype(BF16), mv_ref[:, lo:lo + 128],
                                         preferred_element_type=F32).astype(BF16)


def _attention(q, k, v, xq, mk, mv, lq1, lk1, lq2, lk2, subln, nbatch, seq_len, tq=1024):
    nq = seq_len // tq
    vec = _const_spec((1, DIFF_DH))
    qblk = pl.BlockSpec((tq, 512), lambda b, i: (b * nq + i, 0))
    kv = pl.BlockSpec((seq_len, 512), lambda b, i: (b, 0))
    mem = pl.BlockSpec((N_MEM, 512), lambda b, i: (b, 0))
    return pl.pallas_call(
        _attn_kernel,
        out_shape=[jax.ShapeDtypeStruct(q.shape, BF16)] * 2,
        grid=(nbatch, nq),
        in_specs=[qblk, kv, kv, qblk, mem, mem, vec, vec, vec, vec, _const_spec((1, DIFF_VDH))],
        out_specs=[qblk, qblk],
        scratch_shapes=[pltpu.VMEM((DIFF_HEADS, seq_len, 256), BF16)],
        compiler_params=_params(("parallel", "arbitrary")),
        name="attention",
    )(q, k, v, xq, mk, mv, lq1, lk1, lq2, lk2, subln)


def _merge_kernel(x_ref, u_ref, yr_ref, yd_ref, yx_ref,
                  gmix_ref, wg_ref, d_ref, wglu_ref, bglu_ref, wup_ref, wout_ref, o_ref, ys):
    rows = SEQ_BLK * S5_ROW
    lane_block = lax.broadcasted_iota(jnp.int32, (SEQ_BLK, 128), 1) // S5_GROUP
    for c in range(STEP_BLK // S5_ROW):
        ts = slice(c * S5_ROW, (c + 1) * S5_ROW)
        for hf in range(2):
            for col in range(4):
                outs = _block_transpose8(
                    [yr_ref[8 * col + k, c, :, hf * 128:(hf + 1) * 128] for k in range(8)],
                    lane_block)
                for m in range(8):
                    ys[c, col, pl.ds(hf * 8 + m, SEQ_BLK, stride=S5_ROW), :] = outs[m]

        x = x_ref[:, ts, :].reshape(rows, D_MODEL)
        xn = _rms(x, gmix_ref[...]).astype(BF16)

        y = (jnp.concatenate([ys[c, col] for col in range(4)], axis=-1)
             + d_ref[...] * u_ref[:, ts, :].reshape(rows, 512))
        z = jax.nn.gelu(y)
        zg = jnp.dot(z.astype(BF16), wglu_ref[...], preferred_element_type=F32) + bglu_ref[...]
        y_s5 = z * jax.nn.sigmoid(zg)

        branches = (y_s5.astype(BF16), yd_ref[:, ts, :].reshape(rows, 512),
                    yx_ref[:, ts, :].reshape(rows, 512))
        merged = None
        for n in range(3):
            g = jnp.dot(xn, wg_ref[:, n * D_MODEL:(n + 1) * D_MODEL],
                        preferred_element_type=F32)
            up = jnp.dot(branches[n], wup_ref[n], preferred_element_type=F32)
            term = jax.nn.sigmoid(g) * up
            merged = term if merged is None else merged + term
        out = x + jnp.dot(merged.astype(BF16), wout_ref[...], preferred_element_type=F32)
        o_ref[:, ts, :] = out.reshape(SEQ_BLK, S5_ROW, D_MODEL)


def _merge(x, u, y_rows, yd, yx, norm_mix, w_g, s5_d, w_glu, b_glu, w_up, w_out):
    nb, seq_len, _ = x.shape
    return pl.pallas_call(
        _merge_kernel,
        out_shape=jax.ShapeDtypeStruct((nb, seq_len, D_MODEL), F32),
        grid=(nb // SEQ_BLK, seq_len // STEP_BLK),
        in_specs=[_tok_blk(D_MODEL), _tok_blk(512), _s5_blk(), _tok_blk(512), _tok_blk(512),
                  _const_spec((1, D_MODEL)), _const_spec((D_MODEL, 3 * D_MODEL)),
                  _const_spec((1, 512)), _const_spec((512, 512)), _const_spec((1, 512)),
                  _const_spec((3, BRANCH_WIDTH, D_MODEL)), _const_spec((D_MODEL, D_MODEL))],
        out_specs=_tok_blk(D_MODEL),
        scratch_shapes=[pltpu.VMEM((STEP_BLK // S5_ROW, 4, SEQ_BLK * S5_ROW, 128), F32)],
        compiler_params=_params(("parallel", "parallel")),
        name="merge",
    )(x, u, y_rows, yd, yx, norm_mix, w_g, s5_d, w_glu, b_glu, w_up, w_out)


def _ffn_kernel(x_ref, g_ref, win_ref, wout_ref, gf_ref, o_ref):
    for hb in range(x_ref.shape[0] // FFN_ROWS):
        rs = slice(hb * FFN_ROWS, (hb + 1) * FFN_ROWS)
        x = x_ref[rs, :]
        xn = _rms(x, g_ref[...]).astype(BF16)
        gate = jnp.dot(xn, win_ref[:, 0:D_FF], preferred_element_type=F32)
        up = jnp.dot(xn, win_ref[:, D_FF:2 * D_FF], preferred_element_type=F32)
        act = (jax.nn.silu(gate) * up).astype(BF16)
        x = x + jnp.dot(act, wout_ref[...], preferred_element_type=F32)
        o_ref[rs, :] = _rms(x, gf_ref[...])


def _ffn(x1, norm_ffn, w_in, w_out, norm_final, tm=1024):
    t = x1.shape[0]
    row = lambda i: (i, 0)
    return pl.pallas_call(
        _ffn_kernel,
        out_shape=jax.ShapeDtypeStruct((t, D_MODEL), F32),
        grid=(t // tm,),
        in_specs=[pl.BlockSpec((tm, D_MODEL), row), _const_spec((1, D_MODEL)),
                  _const_spec((D_MODEL, 2 * D_FF)), _const_spec((D_FF, D_MODEL)),
                  _const_spec((1, D_MODEL))],
        out_specs=pl.BlockSpec((tm, D_MODEL), row),
        compiler_params=_params(("parallel",)),
        name="ffn",
    )(x1, norm_ffn, w_in, w_out, norm_final)


def _rope_tables(seq_len):
    half = ROPE_DIM // 2
    inv = 1.0 / (ROPE_THETA ** (jnp.arange(0, ROPE_DIM, 2, dtype=F32) / ROPE_DIM))
    ang = jnp.arange(seq_len, dtype=F32)[:, None] * inv[None, :]
    cos, sin = jnp.cos(ang), jnp.sin(ang)
    pad = jnp.zeros((seq_len, DIFF_DH - ROPE_DIM), F32)
    cos_h = jnp.concatenate([cos, cos, pad + 1.0], axis=-1)
    sa_h = jnp.concatenate([jnp.zeros_like(sin), sin, pad], axis=-1)
    sb_h = jnp.concatenate([-sin, jnp.zeros_like(sin), pad], axis=-1)
    two = lambda a: jnp.concatenate([a, a], axis=-1)
    return two(cos_h), two(sa_h), two(sb_h)


def _trunk(x, mem, w, s5m):
    nb, seq_len, _ = x.shape
    nc = seq_len // S5_ROW
    t = nb * seq_len
    cos_t, sa_t, sb_t = _rope_tables(seq_len)
    u, u_rows, q, k, v, xq = _inproj(x, w["norm_mix"], w["w_a"], cos_t, sa_t, sb_t)
    mk, mv = _memkv(mem.reshape(nb * N_MEM, D_MODEL), w["norm_mem"], w["w_mem_kv"])
    y_rows = _s5_core(u_rows.reshape(S5_GROUPS, nc * nb, S5_LANES), *s5m, nb=nb, nc=nc)
    flat = lambda a: a.reshape(t, 512)
    yd, yx = _attention(flat(q), flat(k), flat(v), flat(xq), mk, mv,
                        w["lq1"], w["lk1"], w["lq2"], w["lk2"], w["subln"], nb, seq_len)
    seq = lambda a: a.reshape(nb, seq_len, 512)
    x1 = _merge(x, u, y_rows.reshape(S5_GROUPS, nc, nb, S5_LANES), seq(yd), seq(yx),
                w["norm_mix"], w["w_g"], w["s5_d"], w["w_glu"], w["b_glu"], w["w_up"], w["w_out"])
    out = _ffn(x1.reshape(t, D_MODEL), w["norm_ffn"], w["w_ffn_in"], w["w_ffn_out"],
               w["norm_final"])
    return out.reshape(nb, seq_len, D_MODEL)


def kernel(x_prompt, x_sample, mem_prompt, mem_sample, norm_mix, norm_mem, w_in, s5_lambda_re, s5_lambda_im, s5_log_step, s5_b_re, s5_b_im, s5_c_re, s5_c_im, s5_d, s5_w_glu, s5_b_glu, diff_lambda_q1, diff_lambda_k1, diff_lambda_q2, diff_lambda_k2, diff_subln, w_mem_kv, w_up, w_out, norm_ffn, w_ffn_in, w_ffn_out, norm_final):
    l = 0
    w = dict(
        norm_mix=norm_mix[l][None, :], norm_mem=norm_mem[l][None, :],
        w_a=w_in[l][:, :A_COLS].astype(BF16), w_g=w_in[l][:, A_COLS:].astype(BF16),
        s5_d=s5_d[l][None, :], w_glu=s5_w_glu[l].astype(BF16), b_glu=s5_b_glu[l][None, :],
        lq1=diff_lambda_q1[l][None, :], lk1=diff_lambda_k1[l][None, :],
        lq2=diff_lambda_q2[l][None, :], lk2=diff_lambda_k2[l][None, :],
        subln=diff_subln[l][None, :], w_mem_kv=w_mem_kv[l].astype(BF16),
        w_up=w_up[l].astype(BF16), w_out=w_out[l].astype(BF16),
        norm_ffn=norm_ffn[l][None, :], w_ffn_in=w_ffn_in[l].astype(BF16),
        w_ffn_out=w_ffn_out[l].astype(BF16), norm_final=norm_final[None, :],
    )
    s5m = _s5_matrices(s5_lambda_re[l], s5_lambda_im[l], s5_log_step[l],
                       s5_b_re[l], s5_b_im[l], s5_c_re[l], s5_c_im[l])
    return (_trunk(x_prompt, mem_prompt, w, s5m), _trunk(x_sample, mem_sample, w, s5m))
```

```python
import functools
import math

import jax
import jax.numpy as jnp
from jax import lax
from jax.experimental import pallas as pl
from jax.experimental.pallas import tpu as pltpu

F32 = jnp.float32
BF16 = jnp.bfloat16

D_MODEL = 1024
N_MEM = 256
EPS = 1e-6
S5_WIDTH = 512
S5_GROUP = 16
S5_GROUPS = 32
S5_STATE = 64
DIFF_HEADS = 4
DIFF_DH = 64
DIFF_VDH = 128
X_HEADS = 4
X_DH = 128
BRANCH_WIDTH = 512
D_FF = 2816
ROPE_THETA = 500000.0
ROPE_DIM = 16
LAMBDA_INIT = 0.8 - 0.6 * math.exp(-0.3 * 0)
LOG2E = 1.4426950408889634
Q_SCALE = DIFF_DH ** -0.5 * LOG2E
A_COLS = 5 * 512
S5_ROW = 16
S5_LANES = S5_ROW * S5_GROUP
SEQ_BLK = 16
STEP_BLK = 64
FFN_ROWS = 256

VMEM_LIMIT = 56 * 1024 * 1024


def _const_spec(shape):
    nd = len(shape)
    return pl.BlockSpec(shape, lambda *_: (0,) * nd, pipeline_mode=pl.Buffered(1))


def _params(sem):
    return pltpu.CompilerParams(dimension_semantics=sem, vmem_limit_bytes=VMEM_LIMIT)


def _rms(x, g):
    return x * lax.rsqrt(jnp.mean(x * x, axis=-1, keepdims=True) + EPS) * g


def _block_transpose8(xs, lane_block):
    rolled = []
    for r in range(8):
        s = xs[(7 + r) % 8]
        for k in range(6, -1, -1):
            s = jnp.where(lane_block == k, xs[(k + r) % 8], s)
        rolled.append(s if r == 0 else pltpu.roll(s, 16 * r, 1))
    outs = []
    for k in range(8):
        o = rolled[(7 - k) % 8]
        for m in range(6, -1, -1):
            o = jnp.where(lane_block == m, rolled[(m - k) % 8], o)
        outs.append(o)
    return outs


def _inproj_kernel(x_ref, g_ref, w_ref, cos_ref, sa_ref, sb_ref,
                   u_ref, ur_ref, q_ref, k_ref, v_ref, xq_ref, us):
    rows = SEQ_BLK * S5_ROW
    lane_block = lax.broadcasted_iota(jnp.int32, (SEQ_BLK, 128), 1) // S5_GROUP
    blk3 = lambda a, w: a.reshape(SEQ_BLK, S5_ROW, w)
    for c in range(STEP_BLK // S5_ROW):
        ts = slice(c * S5_ROW, (c + 1) * S5_ROW)
        xn = _rms(x_ref[:, ts, :].reshape(rows, D_MODEL), g_ref[...]).astype(BF16)
        h = jnp.dot(xn, w_ref[...], preferred_element_type=F32)
        u = h[:, 0:512]
        u_ref[:, ts, :] = blk3(u, 512)
        for col in range(4):
            us[c, col] = u[:, col * 128:(col + 1) * 128]

        for hf in range(2):
            for col in range(4):
                outs = _block_transpose8(
                    [us[c, col, pl.ds(hf * 8 + m, SEQ_BLK, stride=S5_ROW), :] for m in range(8)],
                    lane_block)
                for k in range(8):
                    ur_ref[8 * col + k, c, :, hf * 128:(hf + 1) * 128] = outs[k].astype(BF16)

        tile = lambda t: jnp.broadcast_to(t[None], (SEQ_BLK, S5_ROW, 128)).reshape(rows, 128)
        cos = tile(cos_ref[ts, :])
        sa = tile(sa_ref[ts, :])
        sb = tile(sb_ref[ts, :])
        for j in range(4):
            lo = 128 * j
            tq = h[:, 512 + lo:512 + lo + 128]
            tk = h[:, 1024 + lo:1024 + lo + 128]
            rq = tq * cos + pltpu.roll(tq, 8, 1) * sa + pltpu.roll(tq, 120, 1) * sb
            rk = tk * cos + pltpu.roll(tk, 8, 1) * sa + pltpu.roll(tk, 120, 1) * sb
            q_ref[:, ts, lo:lo + 128] = blk3((rq * Q_SCALE).astype(BF16), 128)
            k_ref[:, ts, lo:lo + 128] = blk3(rk.astype(BF16), 128)
        v_ref[:, ts, :] = blk3(h[:, 1536:2048].astype(BF16), 512)
        xq_ref[:, ts, :] = blk3(h[:, 2048:2560].astype(BF16), 512)


def _tok_blk(width):
    return pl.BlockSpec((SEQ_BLK, STEP_BLK, width), lambda b, t: (b, t, 0))


def _s5_blk():
    return pl.BlockSpec((S5_GROUPS, STEP_BLK // S5_ROW, SEQ_BLK, S5_LANES),
                        lambda b, t: (0, t, b, 0))


def _inproj(x, norm_mix, w_a, cos_t, sa_t, sb_t):
    nb, seq_len, _ = x.shape
    nc = seq_len // S5_ROW
    tab = pl.BlockSpec((STEP_BLK, 128), lambda b, t: (t, 0))
    o512 = lambda dt: jax.ShapeDtypeStruct((nb, seq_len, 512), dt)
    outs = [o512(F32), jax.ShapeDtypeStruct((S5_GROUPS, nc, nb, S5_LANES), BF16)] + [o512(BF16)] * 4
    return pl.pallas_call(
        _inproj_kernel,
        out_shape=outs,
        grid=(nb // SEQ_BLK, seq_len // STEP_BLK),
        in_specs=[_tok_blk(D_MODEL), _const_spec((1, D_MODEL)), _const_spec((D_MODEL, A_COLS)),
                  tab, tab, tab],
        out_specs=[_tok_blk(512), _s5_blk()] + [_tok_blk(512)] * 4,
        scratch_shapes=[pltpu.VMEM((STEP_BLK // S5_ROW, 4, SEQ_BLK * S5_ROW, 128), F32)],
        compiler_params=_params(("parallel", "parallel")),
        name="inproj",
    )(x, norm_mix, w_a, cos_t, sa_t, sb_t)


def _memkv_kernel(m_ref, g_ref, w_ref, mk_ref, mv_ref):
    mn = _rms(m_ref[...], g_ref[...]).astype(BF16)
    h = jnp.dot(mn, w_ref[...], preferred_element_type=F32)
    mk_ref[...] = h[:, 0:512].astype(BF16)
    mv_ref[...] = h[:, 512:1024].astype(BF16)


def _memkv(mem2, norm_mem, w_kv, tm=512):
    t = mem2.shape[0]
    row = lambda i: (i, 0)
    return pl.pallas_call(
        _memkv_kernel,
        out_shape=[jax.ShapeDtypeStruct((t, 512), BF16)] * 2,
        grid=(t // tm,),
        in_specs=[pl.BlockSpec((tm, D_MODEL), row), _const_spec((1, D_MODEL)),
                  _const_spec((D_MODEL, 1024))],
        out_specs=[pl.BlockSpec((tm, 512), row)] * 2,
        compiler_params=_params(("parallel",)),
        name="memkv",
    )(mem2, norm_mem, w_kv)


def _cexp(mag_arg, ang):
    m = jnp.exp(mag_arg)
    return m * jnp.cos(ang), m * jnp.sin(ang)


def _s5_param_kernel(lr_row, li_row, lr_col, li_col, ls_ref, bt_r, bt_i, ct_r, ct_i,
                     m1r_ref, m1i_ref, wr_ref, wi_ref, g2_ref, dr_ref, di_ref):
    hi = lax.Precision.HIGHEST
    s_col = lax.broadcasted_iota(jnp.int32, (S5_ROW, 1), 0).astype(F32)
    j_lane = jnp.minimum(lax.broadcasted_iota(jnp.int32, (1, 128), 1), S5_ROW).astype(F32)
    jj = lax.broadcasted_iota(jnp.int32, (128, S5_LANES), 0)
    ii = lax.broadcasted_iota(jnp.int32, (128, S5_LANES), 1) // S5_GROUP
    rows3 = lambda t: t[:, None, :]
    for d in range(2):
        step = jnp.exp(ls_ref[d])
        lr = lr_row[d]
        li = li_row[d]
        ar = lr * step
        ai = li * step
        lb_r, lb_i = _cexp(ar, ai)
        den = lr * lr + li * li
        nr = lb_r - 1.0
        cf_r = (nr * lr + lb_i * li) / den
        cf_i = (lb_i * lr - nr * li) / den
        btr = bt_r[d]
        bti = bt_i[d]
        bb_r = cf_r * btr - cf_i * bti
        bb_i = cf_r * bti + cf_i * btr
        e = (15.0 - s_col) if d == 0 else s_col
        pw_r, pw_i = _cexp(e * ar, e * ai)
        m1r_ref[d] = (rows3(pw_r) * bb_r[None] - rows3(pw_i) * bb_i[None]).reshape(S5_LANES, 64)
        m1i_ref[d] = (rows3(pw_r) * bb_i[None] + rows3(pw_i) * bb_r[None]).reshape(S5_LANES, 64)
        dr, di = _cexp(16.0 * ar, 16.0 * ai)
        dr_ref[d] = dr
        di_ref[d] = di

        acr = lr_col[d] * step
        aci = li_col[d] * step
        ctr = ct_r[d]
        cti = ct_i[d]
        tb_r, tb_i = _cexp(j_lane * acr, j_lane * aci)
        spread = lambda t, sel: jnp.dot(t, jnp.where(jj == sel, 1.0, 0.0).astype(F32),
                                        preferred_element_type=F32, precision=hi)
        jw = (ii + 1) if d == 0 else (S5_ROW - ii)
        qr = spread(tb_r, jw)
        qi = spread(tb_i, jw)
        wr_ref[d] = ctr * qr - cti * qi
        wi_ref[d] = -(ctr * qi + cti * qr)
        gr = spread(tb_r, ii)
        gi = spread(tb_i, ii)
        cl_r = ctr * gr - cti * gi
        cl_i = ctr * gi + cti * gr
        g2_ref[d] = (jnp.dot(bb_r, cl_r, preferred_element_type=F32, precision=hi)
                     - jnp.dot(bb_i, cl_i, preferred_element_type=F32, precision=hi))


def _s5_matrices(lam_re, lam_im, log_step, b_re, b_im, c_re, c_im):
    G = S5_GROUPS
    lr_row = lam_re.reshape(2, G, 1, S5_STATE)
    li_row = lam_im.reshape(2, G, 1, S5_STATE)
    lr_col = lam_re.reshape(2, G, S5_STATE, 1)
    li_col = lam_im.reshape(2, G, S5_STATE, 1)
    ls = log_step.reshape(2, G, 1, 1)
    bt = lambda b: jnp.swapaxes(b, -1, -2)
    ct = lambda c: jnp.tile(jnp.swapaxes(c, -1, -2), (1, 1, 1, S5_ROW))
    blk = lambda a, b: pl.BlockSpec((2, None, a, b), lambda g: (0, g, 0, 0))
    sd = lambda a, b: jax.ShapeDtypeStruct((2, G, a, b), F32)
    m1r, m1i, wr, wi, g2, dr, di = pl.pallas_call(
        _s5_param_kernel,
        out_shape=[sd(256, 64), sd(256, 64), sd(64, 256), sd(64, 256), sd(16, 256),
                   sd(1, 64), sd(1, 64)],
        grid=(G,),
        in_specs=[blk(1, 64), blk(1, 64), blk(64, 1), blk(64, 1), blk(1, 1),
                  blk(16, 64), blk(16, 64), blk(64, 256), blk(64, 256)],
        out_specs=[blk(256, 64), blk(256, 64), blk(64, 256), blk(64, 256), blk(16, 256),
                   blk(1, 64), blk(1, 64)],
        compiler_params=_params(("parallel",)),
        name="s5_params",
    )(lr_row, li_row, lr_col, li_col, ls, bt(b_re), bt(b_im), ct(c_re), ct(c_im))

    m1 = jnp.concatenate([m1r[0], m1r[1], m1i[0], m1i[1]], axis=-1)
    wa = jnp.concatenate([wr[0], wr[1]], axis=1)
    wb = jnp.concatenate([wi[0], wi[1]], axis=1)
    kf = g2[0].reshape(G, 16, 16, 16)
    kb = g2[1].reshape(G, 16, 16, 16)
    ribbon = jnp.concatenate([kb[:, :, :0:-1], kf[:, :, 0:1] + kb[:, :, 0:1], kf[:, :, 1:]],
                             axis=2)
    tiled = jnp.tile(jnp.pad(ribbon, ((0, 0), (0, 0), (0, 1), (0, 0))), (1, 1, 16, 1))
    cut = tiled[:, :, 15:15 + 16 * 31].reshape(G, 16, 16, 31, 16)[:, :, :, :16]
    tz = cut.transpose(0, 2, 1, 3, 4).reshape(G, 256, 256)
    m2 = jnp.concatenate([tz, wa, wb], axis=1)
    dec_r = jnp.concatenate([dr[0], dr[1]], axis=-1)
    dec_i = jnp.concatenate([di[0], di[1]], axis=-1)
    return m1.astype(BF16), m2.astype(BF16), dec_r, dec_i


def _s5_kernel(u_ref, m1_ref, m2_ref, dr_ref, di_ref, y_ref, ea, eb, st, *, nb, nc):
    e = jnp.dot(u_ref[...], m1_ref[...], preferred_element_type=F32)
    ea[...] = e[:, 0:128]
    eb[...] = e[:, 128:256]
    ar = dr_ref[...]
    ai = di_ref[...]
    fwd = lax.broadcasted_iota(jnp.int32, (nb, 128), 1) < S5_STATE
    h = S5_STATE
    o = S5_LANES
    st[:, 0:o] = u_ref[...]

    def scan(c, carry):
        s_re, s_im = carry
        r0 = c * nb
        r1 = (nc - 1 - c) * nb
        st[pl.ds(r0, nb), o:o + h] = s_re[:, 0:h].astype(BF16)
        st[pl.ds(r1, nb), o + h:o + 2 * h] = s_re[:, h:2 * h].astype(BF16)
        st[pl.ds(r0, nb), o + 2 * h:o + 3 * h] = s_im[:, 0:h].astype(BF16)
        st[pl.ds(r1, nb), o + 3 * h:o + 4 * h] = s_im[:, h:2 * h].astype(BF16)
        in_re = jnp.where(fwd, ea[pl.ds(r0, nb), :], ea[pl.ds(r1, nb), :])
        in_im = jnp.where(fwd, eb[pl.ds(r0, nb), :], eb[pl.ds(r1, nb), :])
        return (ar * s_re - ai * s_im + in_re, ar * s_im + ai * s_re + in_im)

    carry = (jnp.zeros((nb, 128), F32), jnp.zeros((nb, 128), F32))
    for c in range(nc):
        carry = scan(c, carry)
    y_ref[...] = jnp.dot(st[...], m2_ref[...], preferred_element_type=F32)


def _s5_core(u_rows, m1, m2, dec_r, dec_i, nb, nc):
    m = nb * nc
    grp = lambda a, b: pl.BlockSpec((None, a, b), lambda g: (g, 0, 0))
    return pl.pallas_call(
        functools.partial(_s5_kernel, nb=nb, nc=nc),
        out_shape=jax.ShapeDtypeStruct((S5_GROUPS, m, S5_LANES), F32),
        grid=(S5_GROUPS,),
        in_specs=[grp(m, 256), grp(256, 256), grp(512, 256), grp(1, 128), grp(1, 128)],
        out_specs=grp(m, 256),
        scratch_shapes=[pltpu.VMEM((m, 128), F32), pltpu.VMEM((m, 128), F32),
                        pltpu.VMEM((m, 2 * S5_LANES), BF16)],
        compiler_params=_params(("parallel",)),
        name="s5_core",
    )(u_rows, m1, m2, dec_r, dec_i)


def _attn_kernel(q_ref, k_ref, v_ref, xq_ref, mk_ref, mv_ref, lq1, lk1, lq2, lk2, sg_ref,
                 yd_ref, yx_ref, vext):
    seq_len = k_ref.shape[0]

    @pl.when(pl.program_id(1) == 0)
    def _():
        for n in range(DIFF_HEADS):
            vext[n, :, 0:128] = v_ref[:, 128 * n:128 * n + 128]
            vext[n, :, 128:256] = jnp.ones((seq_len, 128), BF16)

    lam = (jnp.exp(jnp.sum(lq1[...] * lk1[...], axis=-1, keepdims=True))
           - jnp.exp(jnp.sum(lq2[...] * lk2[...], axis=-1, keepdims=True))
           + LAMBDA_INIT)
    first = lax.broadcasted_iota(jnp.int32, (1, 128), 1) < DIFF_DH
    zero = jnp.zeros((), BF16)
    for n in range(DIFF_HEADS):
        lo = 128 * n
        q = q_ref[:, lo:lo + 128]
        k = k_ref[:, lo:lo + 128]
        comps = []
        for c in range(2):
            qc = jnp.where(first if c == 0 else jnp.logical_not(first), q, zero)
            s = lax.dot_general(qc, k, (((1,), (1,)), ((), ())), preferred_element_type=F32)
            e = jnp.exp2(s - jnp.max(s, axis=-1, keepdims=True)).astype(BF16)
            pv = jnp.dot(e, vext[n], preferred_element_type=F32)
            comps.append(pv[:, 0:128] / pv[:, 128:256])
        o = comps[0] - lam * comps[1]
        o = _rms(o, sg_ref[...]) * (1.0 - LAMBDA_INIT)
        yd_ref[:, lo:lo + 128] = o.astype(BF16)

    for h in range(X_HEADS):
        lo = 128 * h
        s = lax.dot_general(xq_ref[:, lo:lo + 128], mk_ref[:, lo:lo + 128],
                            (((1,), (1,)), ((), ())), preferred_element_type=F32)
        s = s * (X_DH ** -0.5)
        e = jnp.exp(s - jnp.max(s, axis=-1, keepdims=True))
        p = e * (1.0 / jnp.sum(e, axis=-1, keepdims=True))
        yx_ref[:, lo:lo + 128] = jnp.dot(p.astype(BF16), mv_ref[:, lo:lo + 128],
                                         preferred_element_type=F32).astype(BF16)


def _attention(q, k, v, xq, mk, mv, lq1, lk1, lq2, lk2, subln, nbatch, seq_len, tq=1024):
    nq = seq_len // tq
    vec = _const_spec((1, DIFF_DH))
    qblk = pl.BlockSpec((tq, 512), lambda b, i: (b * nq + i, 0))
    kv = pl.BlockSpec((seq_len, 512), lambda b, i: (b, 0))
    mem = pl.BlockSpec((N_MEM, 512), lambda b, i: (b, 0))
    return pl.pallas_call(
        _attn_kernel,
        out_shape=[jax.ShapeDtypeStruct(q.shape, BF16)] * 2,
        grid=(nbatch, nq),
        in_specs=[qblk, kv, kv, qblk, mem, mem, vec, vec, vec, vec, _const_spec((1, DIFF_VDH))],
        out_specs=[qblk, qblk],
        scratch_shapes=[pltpu.VMEM((DIFF_HEADS, seq_len, 256), BF16)],
        compiler_params=_params(("parallel", "arbitrary")),
        name="attention",
    )(q, k, v, xq, mk, mv, lq1, lk1, lq2, lk2, subln)


def _merge_kernel(x_ref, u_ref, yr_ref, yd_ref, yx_ref,
                  gmix_ref, wg_ref, d_ref, wglu_ref, bglu_ref, wup_ref, wout_ref, o_ref, ys):
    rows = SEQ_BLK * S5_ROW
    lane_block = lax.broadcasted_iota(jnp.int32, (SEQ_BLK, 128), 1) // S5_GROUP
    for c in range(STEP_BLK // S5_ROW):
        ts = slice(c * S5_ROW, (c + 1) * S5_ROW)
        for hf in range(2):
            for col in range(4):
                outs = _block_transpose8(
                    [yr_ref[8 * col + k, c, :, hf * 128:(hf + 1) * 128] for k in range(8)],
                    lane_block)
                for m in range(8):
                    ys[c, col, pl.ds(hf * 8 + m, SEQ_BLK, stride=S5_ROW), :] = outs[m]

        x = x_ref[:, ts, :].reshape(rows, D_MODEL)
        xn = _rms(x, gmix_ref[...]).astype(BF16)

        y = (jnp.concatenate([ys[c, col] for col in range(4)], axis=-1)
             + d_ref[...] * u_ref[:, ts, :].reshape(rows, 512))
        z = jax.nn.gelu(y)
        zg = jnp.dot(z.astype(BF16), wglu_ref[...], preferred_element_type=F32) + bglu_ref[...]
        y_s5 = z * jax.nn.sigmoid(zg)

        branches = (y_s5.astype(BF16), yd_ref[:, ts, :].reshape(rows, 512),
                    yx_ref[:, ts, :].reshape(rows, 512))
        merged = None
        for n in range(3):
            g = jnp.dot(xn, wg_ref[:, n * D_MODEL:(n + 1) * D_MODEL],
                        preferred_element_type=F32)
            up = jnp.dot(branches[n], wup_ref[n], preferred_element_type=F32)
            term = jax.nn.sigmoid(g) * up
            merged = term if merged is None else merged + term
        out = x + jnp.dot(merged.astype(BF16), wout_ref[...], preferred_element_type=F32)
        o_ref[:, ts, :] = out.reshape(SEQ_BLK, S5_ROW, D_MODEL)


def _merge(x, u, y_rows, yd, yx, norm_mix, w_g, s5_d, w_glu, b_glu, w_up, w_out):
    nb, seq_len, _ = x.shape
    return pl.pallas_call(
        _merge_kernel,
        out_shape=jax.ShapeDtypeStruct((nb, seq_len, D_MODEL), F32),
        grid=(nb // SEQ_BLK, seq_len // STEP_BLK),
        in_specs=[_tok_blk(D_MODEL), _tok_blk(512), _s5_blk(), _tok_blk(512), _tok_blk(512),
                  _const_spec((1, D_MODEL)), _const_spec((D_MODEL, 3 * D_MODEL)),
                  _const_spec((1, 512)), _const_spec((512, 512)), _const_spec((1, 512)),
                  _const_spec((3, BRANCH_WIDTH, D_MODEL)), _const_spec((D_MODEL, D_MODEL))],
        out_specs=_tok_blk(D_MODEL),
        scratch_shapes=[pltpu.VMEM((STEP_BLK // S5_ROW, 4, SEQ_BLK * S5_ROW, 128), F32)],
        compiler_params=_params(("parallel", "parallel")),
        name="merge",
    )(x, u, y_rows, yd, yx, norm_mix, w_g, s5_d, w_glu, b_glu, w_up, w_out)


def _ffn_kernel(x_ref, g_ref, win_ref, wout_ref, gf_ref, o_ref):
    for hb in range(x_ref.shape[0] // FFN_ROWS):
        rs = slice(hb * FFN_ROWS, (hb + 1) * FFN_ROWS)
        x = x_ref[rs, :]
        xn = _rms(x, g_ref[...]).astype(BF16)
        gate = jnp.dot(xn, win_ref[:, 0:D_FF], preferred_element_type=F32)
        up = jnp.dot(xn, win_ref[:, D_FF:2 * D_FF], preferred_element_type=F32)
        act = (jax.nn.silu(gate) * up).astype(BF16)
        x = x + jnp.dot(act, wout_ref[...], preferred_element_type=F32)
        o_ref[rs, :] = _rms(x, gf_ref[...])


def _ffn(x1, norm_ffn, w_in, w_out, norm_final, tm=1024):
    t = x1.shape[0]
    row = lambda i: (i, 0)
    return pl.pallas_call(
        _ffn_kernel,
        out_shape=jax.ShapeDtypeStruct((t, D_MODEL), F32),
        grid=(t // tm,),
        in_specs=[pl.BlockSpec((tm, D_MODEL), row), _const_spec((1, D_MODEL)),
                  _const_spec((D_MODEL, 2 * D_FF)), _const_spec((D_FF, D_MODEL)),
                  _const_spec((1, D_MODEL))],
        out_specs=pl.BlockSpec((tm, D_MODEL), row),
        compiler_params=_params(("parallel",)),
        name="ffn",
    )(x1, norm_ffn, w_in, w_out, norm_final)


def _rope_tables(seq_len):
    half = ROPE_DIM // 2
    inv = 1.0 / (ROPE_THETA ** (jnp.arange(0, ROPE_DIM, 2, dtype=F32) / ROPE_DIM))
    ang = jnp.arange(seq_len, dtype=F32)[:, None] * inv[None, :]
    cos, sin = jnp.cos(ang), jnp.sin(ang)
    pad = jnp.zeros((seq_len, DIFF_DH - ROPE_DIM), F32)
    cos_h = jnp.concatenate([cos, cos, pad + 1.0], axis=-1)
    sa_h = jnp.concatenate([jnp.zeros_like(sin), sin, pad], axis=-1)
    sb_h = jnp.concatenate([-sin, jnp.zeros_like(sin), pad], axis=-1)
    two = lambda a: jnp.concatenate([a, a], axis=-1)
    return two(cos_h), two(sa_h), two(sb_h)


def _trunk(x, mem, w, s5m):
    nb, seq_len, _ = x.shape
    nc = seq_len // S5_ROW
    t = nb * seq_len
    cos_t, sa_t, sb_t = _rope_tables(seq_len)
    u, u_rows, q, k, v, xq = _inproj(x, w["norm_mix"], w["w_a"], cos_t, sa_t, sb_t)
    mk, mv = _memkv(mem.reshape(nb * N_MEM, D_MODEL), w["norm_mem"], w["w_mem_kv"])
    y_rows = _s5_core(u_rows.reshape(S5_GROUPS, nc * nb, S5_LANES), *s5m, nb=nb, nc=nc)
    flat = lambda a: a.reshape(t, 512)
    yd, yx = _attention(flat(q), flat(k), flat(v), flat(xq), mk, mv,
                        w["lq1"], w["lk1"], w["lq2"], w["lk2"], w["subln"], nb, seq_len)
    seq = lambda a: a.reshape(nb, seq_len, 512)
    x1 = _merge(x, u, y_rows.reshape(S5_GROUPS, nc, nb, S5_LANES), seq(yd), seq(yx),
                w["norm_mix"], w["w_g"], w["s5_d"], w["w_glu"], w["b_glu"], w["w_up"], w["w_out"])
    out = _ffn(x1.reshape(t, D_MODEL), w["norm_ffn"], w["w_ffn_in"], w["w_ffn_out"],
               w["norm_final"])
    return out.reshape(nb, seq_len, D_MODEL)


def kernel(x_prompt, x_sample, mem_prompt, mem_sample, norm_mix, norm_mem, w_in, s5_lambda_re, s5_lambda_im, s5_log_step, s5_b_re, s5_b_im, s5_c_re, s5_c_im, s5_d, s5_w_glu, s5_b_glu, diff_lambda_q1, diff_lambda_k1, diff_lambda_q2, diff_lambda_k2, diff_subln, w_mem_kv, w_up, w_out, norm_ffn, w_ffn_in, w_ffn_out, norm_final):
    l = 0
    w = dict(
        norm_mix=norm_mix[l][None, :], norm_mem=norm_mem[l][None, :],
        w_a=w_in[l][:, :A_COLS].astype(BF16), w_g=w_in[l][:, A_COLS:].astype(BF16),
        s5_d=s5_d[l][None, :], w_glu=s5_w_glu[l].astype(BF16), b_glu=s5_b_glu[l][None, :],
        lq1=diff_lambda_q1[l][None, :], lk1=diff_lambda_k1[l][None, :],
        lq2=diff_lambda_q2[l][None, :], lk2=diff_lambda_k2[l][None, :],
        subln=diff_subln[l][None, :], w_mem_kv=w_mem_kv[l].astype(BF16),
        w_up=w_up[l].astype(BF16), w_out=w_out[l].astype(BF16),
        norm_ffn=norm_ffn[l][None, :], w_ffn_in=w_ffn_in[l].astype(BF16),
        w_ffn_out=w_ffn_out[l].astype(BF16), norm_final=norm_final[None, :],
    )
    s5m = _s5_matrices(s5_lambda_re[l], s5_lambda_im[l], s5_log_step[l],
                       s5_b_re[l], s5_b_im[l], s5_c_re[l], s5_c_im[l])
    return (_trunk(x_prompt, mem_prompt, w, s5m), _trunk(x_sample, mem_sample, w, s5m))
```

```python
import functools
import math

import jax
import jax.numpy as jnp
from jax import lax
from jax.experimental import pallas as pl
from jax.experimental.pallas import tpu as pltpu

F32 = jnp.float32
BF16 = jnp.bfloat16

D_MODEL = 1024
N_MEM = 256
EPS = 1e-6
S5_WIDTH = 512
S5_GROUP = 16
S5_GROUPS = 32
S5_STATE = 64
DIFF_HEADS = 4
DIFF_DH = 64
DIFF_VDH = 128
X_HEADS = 4
X_DH = 128
BRANCH_WIDTH = 512
D_FF = 2816
ROPE_THETA = 500000.0
ROPE_DIM = 16
LAMBDA_INIT = 0.8 - 0.6 * math.exp(-0.3 * 0)
LOG2E = 1.4426950408889634
Q_SCALE = DIFF_DH ** -0.5 * LOG2E
XQ_SCALE = X_DH ** -0.5 * LOG2E
A_COLS = 5 * 512
S5_ROW = 16
S5_LANES = S5_ROW * S5_GROUP
SEQ_BLK = 16
STEP_BLK = 64
FFN_ROWS = 256

VMEM_LIMIT = 56 * 1024 * 1024


def _const_spec(shape):
    nd = len(shape)
    return pl.BlockSpec(shape, lambda *_: (0,) * nd, pipeline_mode=pl.Buffered(1))


def _params(sem):
    return pltpu.CompilerParams(dimension_semantics=sem, vmem_limit_bytes=VMEM_LIMIT)


def _rms(x, g):
    return x * lax.rsqrt(jnp.mean(x * x, axis=-1, keepdims=True) + EPS) * g


def _block_transpose8(xs, lane_block):
    rolled = []
    for r in range(8):
        s = xs[(7 + r) % 8]
        for k in range(6, -1, -1):
            s = jnp.where(lane_block == k, xs[(k + r) % 8], s)
        rolled.append(s if r == 0 else pltpu.roll(s, 16 * r, 1))
    outs = []
    for k in range(8):
        o = rolled[(7 - k) % 8]
        for m in range(6, -1, -1):
            o = jnp.where(lane_block == m, rolled[(m - k) % 8], o)
        outs.append(o)
    return outs


def _inproj_kernel(x_ref, g_ref, w_ref, cos_ref, sa_ref, sb_ref,
                   u_ref, ur_ref, q_ref, k_ref, v_ref, xq_ref, us):
    rows = SEQ_BLK * S5_ROW
    lane_block = lax.broadcasted_iota(jnp.int32, (SEQ_BLK, 128), 1) // S5_GROUP
    blk3 = lambda a, w: a.reshape(SEQ_BLK, S5_ROW, w)
    for c in range(STEP_BLK // S5_ROW):
        ts = slice(c * S5_ROW, (c + 1) * S5_ROW)
        xn = _rms(x_ref[:, ts, :].reshape(rows, D_MODEL), g_ref[...]).astype(BF16)
        h = jnp.dot(xn, w_ref[...], preferred_element_type=F32)
        u = h[:, 0:512]
        u_ref[:, ts, :] = blk3(u, 512)
        for col in range(4):
            us[c, col] = u[:, col * 128:(col + 1) * 128]

        for hf in range(2):
            for col in range(4):
                outs = _block_transpose8(
                    [us[c, col, pl.ds(hf * 8 + m, SEQ_BLK, stride=S5_ROW), :] for m in range(8)],
                    lane_block)
                for k in range(8):
                    ur_ref[8 * col + k, c, :, hf * 128:(hf + 1) * 128] = outs[k].astype(BF16)

        tile = lambda t: jnp.broadcast_to(t[None], (SEQ_BLK, S5_ROW, 128)).reshape(rows, 128)
        cos = tile(cos_ref[ts, :])
        sa = tile(sa_ref[ts, :])
        sb = tile(sb_ref[ts, :])
        for j in range(4):
            lo = 128 * j
            tq = h[:, 512 + lo:512 + lo + 128]
            tk = h[:, 1024 + lo:1024 + lo + 128]
            rq = tq * cos + pltpu.roll(tq, 8, 1) * sa + pltpu.roll(tq, 120, 1) * sb
            rk = tk * cos + pltpu.roll(tk, 8, 1) * sa + pltpu.roll(tk, 120, 1) * sb
            q_ref[:, ts, lo:lo + 128] = blk3((rq * Q_SCALE).astype(BF16), 128)
            k_ref[:, ts, lo:lo + 128] = blk3(rk.astype(BF16), 128)
        v_ref[:, ts, :] = blk3(h[:, 1536:2048].astype(BF16), 512)
        xq_ref[:, ts, :] = blk3((h[:, 2048:2560] * XQ_SCALE).astype(BF16), 512)


def _tok_blk(width):
    return pl.BlockSpec((SEQ_BLK, STEP_BLK, width), lambda b, t: (b, t, 0))


def _s5_blk():
    return pl.BlockSpec((S5_GROUPS, STEP_BLK // S5_ROW, SEQ_BLK, S5_LANES),
                        lambda b, t: (0, t, b, 0))


def _inproj(x, norm_mix, w_a, cos_t, sa_t, sb_t):
    nb, seq_len, _ = x.shape
    nc = seq_len // S5_ROW
    tab = pl.BlockSpec((STEP_BLK, 128), lambda b, t: (t, 0))
    o512 = lambda dt: jax.ShapeDtypeStruct((nb, seq_len, 512), dt)
    outs = [o512(F32), jax.ShapeDtypeStruct((S5_GROUPS, nc, nb, S5_LANES), BF16)] + [o512(BF16)] * 4
    return pl.pallas_call(
        _inproj_kernel,
        out_shape=outs,
        grid=(nb // SEQ_BLK, seq_len // STEP_BLK),
        in_specs=[_tok_blk(D_MODEL), _const_spec((1, D_MODEL)), _const_spec((D_MODEL, A_COLS)),
                  tab, tab, tab],
        out_specs=[_tok_blk(512), _s5_blk()] + [_tok_blk(512)] * 4,
        scratch_shapes=[pltpu.VMEM((STEP_BLK // S5_ROW, 4, SEQ_BLK * S5_ROW, 128), F32)],
        compiler_params=_params(("parallel", "parallel")),
        name="inproj",
    )(x, norm_mix, w_a, cos_t, sa_t, sb_t)


def _memkv_kernel(m_ref, g_ref, w_ref, mk_ref, mv_ref):
    mn = _rms(m_ref[...], g_ref[...]).astype(BF16)
    h = jnp.dot(mn, w_ref[...], preferred_element_type=F32)
    mk_ref[...] = h[:, 0:512].astype(BF16)
    for n in range(X_HEADS):
        mv_ref[:, 256 * n:256 * n + 128] = h[:, 512 + 128 * n:512 + 128 * n + 128].astype(BF16)
        mv_ref[:, 256 * n + 128:256 * n + 256] = jnp.ones((m_ref.shape[0], 128), BF16)


def _memkv(mem2, norm_mem, w_kv, tm=512):
    t = mem2.shape[0]
    row = lambda i: (i, 0)
    return pl.pallas_call(
        _memkv_kernel,
        out_shape=[jax.ShapeDtypeStruct((t, 512), BF16), jax.ShapeDtypeStruct((t, 1024), BF16)],
        grid=(t // tm,),
        in_specs=[pl.BlockSpec((tm, D_MODEL), row), _const_spec((1, D_MODEL)),
                  _const_spec((D_MODEL, 1024))],
        out_specs=[pl.BlockSpec((tm, 512), row), pl.BlockSpec((tm, 1024), row)],
        compiler_params=_params(("parallel",)),
        name="memkv",
    )(mem2, norm_mem, w_kv)


def _cexp(mag_arg, ang):
    m = jnp.exp(mag_arg)
    return m * jnp.cos(ang), m * jnp.sin(ang)


def _s5_param_kernel(lr_row, li_row, lr_col, li_col, ls_ref, bt_r, bt_i, ct_r, ct_i,
                     m1r_ref, m1i_ref, wr_ref, wi_ref, g2_ref, dr_ref, di_ref):
    hi = lax.Precision.HIGHEST
    s_col = lax.broadcasted_iota(jnp.int32, (S5_ROW, 1), 0).astype(F32)
    j_lane = jnp.minimum(lax.broadcasted_iota(jnp.int32, (1, 128), 1), S5_ROW).astype(F32)
    jj = lax.broadcasted_iota(jnp.int32, (128, S5_LANES), 0)
    ii = lax.broadcasted_iota(jnp.int32, (128, S5_LANES), 1) // S5_GROUP
    rows3 = lambda t: t[:, None, :]
    for d in range(2):
        step = jnp.exp(ls_ref[d])
        lr = lr_row[d]
        li = li_row[d]
        ar = lr * step
        ai = li * step
        lb_r, lb_i = _cexp(ar, ai)
        den = lr * lr + li * li
        nr = lb_r - 1.0
        cf_r = (nr * lr + lb_i * li) / den
        cf_i = (lb_i * lr - nr * li) / den
        btr = bt_r[d]
        bti = bt_i[d]
        bb_r = cf_r * btr - cf_i * bti
        bb_i = cf_r * bti + cf_i * btr
        e = (15.0 - s_col) if d == 0 else s_col
        pw_r, pw_i = _cexp(e * ar, e * ai)
        m1r_ref[d] = (rows3(pw_r) * bb_r[None] - rows3(pw_i) * bb_i[None]).reshape(S5_LANES, 64)
        m1i_ref[d] = (rows3(pw_r) * bb_i[None] + rows3(pw_i) * bb_r[None]).reshape(S5_LANES, 64)
        dr, di = _cexp(16.0 * ar, 16.0 * ai)
        dr_ref[d] = dr
        di_ref[d] = di

        acr = lr_col[d] * step
        aci = li_col[d] * step
        ctr = ct_r[d]
        cti = ct_i[d]
        tb_r, tb_i = _cexp(j_lane * acr, j_lane * aci)
        spread = lambda t, sel: jnp.dot(t, jnp.where(jj == sel, 1.0, 0.0).astype(F32),
                                        preferred_element_type=F32, precision=hi)
        jw = (ii + 1) if d == 0 else (S5_ROW - ii)
        qr = spread(tb_r, jw)
        qi = spread(tb_i, jw)
        wr_ref[d] = ctr * qr - cti * qi
        wi_ref[d] = -(ctr * qi + cti * qr)
        gr = spread(tb_r, ii)
        gi = spread(tb_i, ii)
        cl_r = ctr * gr - cti * gi
        cl_i = ctr * gi + cti * gr
        g2_ref[d] = (jnp.dot(bb_r, cl_r, preferred_element_type=F32, precision=hi)
                     - jnp.dot(bb_i, cl_i, preferred_element_type=F32, precision=hi))


def _s5_matrices(lam_re, lam_im, log_step, b_re, b_im, c_re, c_im):
    G = S5_GROUPS
    lr_row = lam_re.reshape(2, G, 1, S5_STATE)
    li_row = lam_im.reshape(2, G, 1, S5_STATE)
    lr_col = lam_re.reshape(2, G, S5_STATE, 1)
    li_col = lam_im.reshape(2, G, S5_STATE, 1)
    ls = log_step.reshape(2, G, 1, 1)
    bt = lambda b: jnp.swapaxes(b, -1, -2)
    ct = lambda c: jnp.tile(jnp.swapaxes(c, -1, -2), (1, 1, 1, S5_ROW))
    blk = lambda a, b: pl.BlockSpec((2, None, a, b), lambda g: (0, g, 0, 0))
    sd = lambda a, b: jax.ShapeDtypeStruct((2, G, a, b), F32)
    m1r, m1i, wr, wi, g2, dr, di = pl.pallas_call(
        _s5_param_kernel,
        out_shape=[sd(256, 64), sd(256, 64), sd(64, 256), sd(64, 256), sd(16, 256),
                   sd(1, 64), sd(1, 64)],
        grid=(G,),
        in_specs=[blk(1, 64), blk(1, 64), blk(64, 1), blk(64, 1), blk(1, 1),
                  blk(16, 64), blk(16, 64), blk(64, 256), blk(64, 256)],
        out_specs=[blk(256, 64), blk(256, 64), blk(64, 256), blk(64, 256), blk(16, 256),
                   blk(1, 64), blk(1, 64)],
        compiler_params=_params(("parallel",)),
        name="s5_params",
    )(lr_row, li_row, lr_col, li_col, ls, bt(b_re), bt(b_im), ct(c_re), ct(c_im))

    m1 = jnp.concatenate([m1r[0], m1r[1], m1i[0], m1i[1]], axis=-1)
    wa = jnp.concatenate([wr[0], wr[1]], axis=1)
    wb = jnp.concatenate([wi[0], wi[1]], axis=1)
    kf = g2[0].reshape(G, 16, 16, 16)
    kb = g2[1].reshape(G, 16, 16, 16)
    ribbon = jnp.concatenate([kb[:, :, :0:-1], kf[:, :, 0:1] + kb[:, :, 0:1], kf[:, :, 1:]],
                             axis=2).reshape(G, 16, 31 * 16)
    tz = jnp.stack([ribbon[:, :, 16 * (15 - s):16 * (15 - s) + 256] for s in range(16)],
                   axis=1).reshape(G, 256, 256)
    m2 = jnp.concatenate([tz, wa, wb], axis=1)
    dec_r = jnp.concatenate([dr[0], dr[1]], axis=-1)
    dec_i = jnp.concatenate([di[0], di[1]], axis=-1)
    return m1.astype(BF16), m2.astype(BF16), dec_r, dec_i


def _s5_kernel(u_ref, m1_ref, m2_ref, dr_ref, di_ref, y_ref, ea, eb, st, *, nb, nc):
    e = jnp.dot(u_ref[...], m1_ref[...], preferred_element_type=F32)
    ea[...] = e[:, 0:128]
    eb[...] = e[:, 128:256]
    ar = dr_ref[...]
    ai = di_ref[...]
    fwd = lax.broadcasted_iota(jnp.int32, (nb, 128), 1) < S5_STATE
    h = S5_STATE

    def scan(c, carry):
        s_re, s_im = carry
        r0 = c * nb
        r1 = (nc - 1 - c) * nb
        st[pl.ds(r0, nb), 0:h] = s_re[:, 0:h].astype(BF16)
        st[pl.ds(r1, nb), h:2 * h] = s_re[:, h:2 * h].astype(BF16)
        st[pl.ds(r0, nb), 2 * h:3 * h] = s_im[:, 0:h].astype(BF16)
        st[pl.ds(r1, nb), 3 * h:4 * h] = s_im[:, h:2 * h].astype(BF16)
        in_re = jnp.where(fwd, ea[pl.ds(r0, nb), :], ea[pl.ds(r1, nb), :])
        in_im = jnp.where(fwd, eb[pl.ds(r0, nb), :], eb[pl.ds(r1, nb), :])
        return (ar * s_re - ai * s_im + in_re, ar * s_im + ai * s_re + in_im)

    y = jnp.dot(u_ref[...], m2_ref[0:256, :], preferred_element_type=F32)
    carry = (jnp.zeros((nb, 128), F32), jnp.zeros((nb, 128), F32))
    for c in range(nc):
        carry = scan(c, carry)
    y += jnp.dot(st[...], m2_ref[256:512, :], preferred_element_type=F32)
    y_ref[...] = y


def _s5_core(u_rows, m1, m2, dec_r, dec_i, nb, nc):
    m = nb * nc
    grp = lambda a, b: pl.BlockSpec((None, a, b), lambda g: (g, 0, 0))
    return pl.pallas_call(
        functools.partial(_s5_kernel, nb=nb, nc=nc),
        out_shape=jax.ShapeDtypeStruct((S5_GROUPS, m, S5_LANES), F32),
        grid=(S5_GROUPS,),
        in_specs=[grp(m, 256), grp(256, 256), grp(512, 256), grp(1, 128), grp(1, 128)],
        out_specs=grp(m, 256),
        scratch_shapes=[pltpu.VMEM((m, 128), F32), pltpu.VMEM((m, 128), F32),
                        pltpu.VMEM((m, S5_LANES), BF16)],
        compiler_params=_params(("parallel",)),
        name="s5_core",
    )(u_rows, m1, m2, dec_r, dec_i)


def _attn_kernel(q_ref, k_ref, v_ref, xq_ref, mk_ref, mv_ref, lq1, lk1, lq2, lk2, sg_ref,
                 yd_ref, yx_ref, vext):
    seq_len = k_ref.shape[0]

    @pl.when(pl.program_id(1) == 0)
    def _():
        for n in range(DIFF_HEADS):
            vext[n, :, 0:128] = v_ref[:, 128 * n:128 * n + 128]
            vext[n, :, 128:256] = jnp.ones((seq_len, 128), BF16)

    lam = (jnp.exp(jnp.sum(lq1[...] * lk1[...], axis=-1, keepdims=True))
           - jnp.exp(jnp.sum(lq2[...] * lk2[...], axis=-1, keepdims=True))
           + LAMBDA_INIT)
    first = lax.broadcasted_iota(jnp.int32, (1, 128), 1) < DIFF_DH
    zero = jnp.zeros((), BF16)
    for n in range(DIFF_HEADS):
        lo = 128 * n
        q = q_ref[:, lo:lo + 128]
        k = k_ref[:, lo:lo + 128]
        comps = []
        for c in range(2):
            qc = jnp.where(first if c == 0 else jnp.logical_not(first), q, zero)
            s = lax.dot_general(qc, k, (((1,), (1,)), ((), ())), preferred_element_type=F32)
            e = jnp.exp2(s - jnp.max(s, axis=-1, keepdims=True)).astype(BF16)
            pv = jnp.dot(e, vext[n], preferred_element_type=F32)
            comps.append(pv[:, 0:128] / pv[:, 128:256])
        o = comps[0] - lam * comps[1]
        o = _rms(o, sg_ref[...]) * (1.0 - LAMBDA_INIT)
        yd_ref[:, lo:lo + 128] = o.astype(BF16)

    for h in range(X_HEADS):
        lo = 128 * h
        s = lax.dot_general(xq_ref[:, lo:lo + 128], mk_ref[:, lo:lo + 128],
                            (((1,), (1,)), ((), ())), preferred_element_type=F32)
        e = jnp.exp2(s - jnp.max(s, axis=-1, keepdims=True)).astype(BF16)
        pv = jnp.dot(e, mv_ref[:, 2 * lo:2 * lo + 256], preferred_element_type=F32)
        yx_ref[:, lo:lo + 128] = (pv[:, 0:128] / pv[:, 128:256]).astype(BF16)


def _attention(q, k, v, xq, mk, mv, lq1, lk1, lq2, lk2, subln, nbatch, seq_len, tq=1024):
    nq = seq_len // tq
    vec = _const_spec((1, DIFF_DH))
    qblk = pl.BlockSpec((tq, 512), lambda b, i: (b * nq + i, 0))
    kv = pl.BlockSpec((seq_len, 512), lambda b, i: (b, 0))
    mem = lambda w: pl.BlockSpec((N_MEM, w), lambda b, i: (b, 0))
    return pl.pallas_call(
        _attn_kernel,
        out_shape=[jax.ShapeDtypeStruct(q.shape, BF16)] * 2,
        grid=(nbatch, nq),
        in_specs=[qblk, kv, kv, qblk, mem(512), mem(1024), vec, vec, vec, vec,
                  _const_spec((1, DIFF_VDH))],
        out_specs=[qblk, qblk],
        scratch_shapes=[pltpu.VMEM((DIFF_HEADS, seq_len, 256), BF16)],
        compiler_params=_params(("parallel", "arbitrary")),
        name="attention",
    )(q, k, v, xq, mk, mv, lq1, lk1, lq2, lk2, subln)


def _merge_kernel(x_ref, u_ref, yr_ref, yd_ref, yx_ref,
                  gmix_ref, wg_ref, d_ref, wglu_ref, bglu_ref, wup_ref, wout_ref, o_ref, ys):
    rows = SEQ_BLK * S5_ROW
    lane_block = lax.broadcasted_iota(jnp.int32, (SEQ_BLK, 128), 1) // S5_GROUP
    for c in range(STEP_BLK // S5_ROW):
        ts = slice(c * S5_ROW, (c + 1) * S5_ROW)
        for hf in range(2):
            for col in range(4):
                outs = _block_transpose8(
                    [yr_ref[8 * col + k, c, :, hf * 128:(hf + 1) * 128] for k in range(8)],
                    lane_block)
                for m in range(8):
                    ys[c, col, pl.ds(hf * 8 + m, SEQ_BLK, stride=S5_ROW), :] = outs[m]

        x = x_ref[:, ts, :].reshape(rows, D_MODEL)
        xn = _rms(x, gmix_ref[...]).astype(BF16)

        y = (jnp.concatenate([ys[c, col] for col in range(4)], axis=-1)
             + d_ref[...] * u_ref[:, ts, :].reshape(rows, 512))
        z = jax.nn.gelu(y)
        zg = jnp.dot(z.astype(BF16), wglu_ref[...], preferred_element_type=F32) + bglu_ref[...]
        y_s5 = z * jax.nn.sigmoid(zg)

        branches = (y_s5.astype(BF16), yd_ref[:, ts, :].reshape(rows, 512),
                    yx_ref[:, ts, :].reshape(rows, 512))
        merged = None
        for n in range(3):
            g = jnp.dot(xn, wg_ref[:, n * D_MODEL:(n + 1) * D_MODEL],
                        preferred_element_type=F32)
            up = jnp.dot(branches[n], wup_ref[n], preferred_element_type=F32)
            term = jax.nn.sigmoid(g) * up
            merged = term if merged is None else merged + term
        out = x + jnp.dot(merged.astype(BF16), wout_ref[...], preferred_element_type=F32)
        o_ref[:, ts, :] = out.reshape(SEQ_BLK, S5_ROW, D_MODEL)


def _merge(x, u, y_rows, yd, yx, norm_mix, w_g, s5_d, w_glu, b_glu, w_up, w_out):
    nb, seq_len, _ = x.shape
    return pl.pallas_call(
        _merge_kernel,
        out_shape=jax.ShapeDtypeStruct((nb, seq_len, D_MODEL), F32),
        grid=(nb // SEQ_BLK, seq_len // STEP_BLK),
        in_specs=[_tok_blk(D_MODEL), _tok_blk(512), _s5_blk(), _tok_blk(512), _tok_blk(512),
                  _const_spec((1, D_MODEL)), _const_spec((D_MODEL, 3 * D_MODEL)),
                  _const_spec((1, 512)), _const_spec((512, 512)), _const_spec((1, 512)),
                  _const_spec((3, BRANCH_WIDTH, D_MODEL)), _const_spec((D_MODEL, D_MODEL))],
        out_specs=_tok_blk(D_MODEL),
        scratch_shapes=[pltpu.VMEM((STEP_BLK // S5_ROW, 4, SEQ_BLK * S5_ROW, 128), F32)],
        compiler_params=_params(("parallel", "parallel")),
        name="merge",
    )(x, u, y_rows, yd, yx, norm_mix, w_g, s5_d, w_glu, b_glu, w_up, w_out)


def _ffn_kernel(x_ref, g_ref, win_ref, wout_ref, gf_ref, o_ref):
    for hb in range(x_ref.shape[0] // FFN_ROWS):
        rs = slice(hb * FFN_ROWS, (hb + 1) * FFN_ROWS)
        x = x_ref[rs, :]
        xn = _rms(x, g_ref[...]).astype(BF16)
        gate = jnp.dot(xn, win_ref[:, 0:D_FF], preferred_element_type=F32)
        up = jnp.dot(xn, win_ref[:, D_FF:2 * D_FF], preferred_element_type=F32)
        act = (jax.nn.silu(gate) * up).astype(BF16)
        x = x + jnp.dot(act, wout_ref[...], preferred_element_type=F32)
        o_ref[rs, :] = _rms(x, gf_ref[...])


def _ffn(x1, norm_ffn, w_in, w_out, norm_final, tm=1024):
    t = x1.shape[0]
    row = lambda i: (i, 0)
    return pl.pallas_call(
        _ffn_kernel,
        out_shape=jax.ShapeDtypeStruct((t, D_MODEL), F32),
        grid=(t // tm,),
        in_specs=[pl.BlockSpec((tm, D_MODEL), row), _const_spec((1, D_MODEL)),
                  _const_spec((D_MODEL, 2 * D_FF)), _const_spec((D_FF, D_MODEL)),
                  _const_spec((1, D_MODEL))],
        out_specs=pl.BlockSpec((tm, D_MODEL), row),
        compiler_params=_params(("parallel",)),
        name="ffn",
    )(x1, norm_ffn, w_in, w_out, norm_final)


def _rope_tables(seq_len):
    half = ROPE_DIM // 2
    inv = 1.0 / (ROPE_THETA ** (jnp.arange(0, ROPE_DIM, 2, dtype=F32) / ROPE_DIM))
    ang = jnp.arange(seq_len, dtype=F32)[:, None] * inv[None, :]
    cos, sin = jnp.cos(ang), jnp.sin(ang)
    pad = jnp.zeros((seq_len, DIFF_DH - ROPE_DIM), F32)
    cos_h = jnp.concatenate([cos, cos, pad + 1.0], axis=-1)
    sa_h = jnp.concatenate([jnp.zeros_like(sin), sin, pad], axis=-1)
    sb_h = jnp.concatenate([-sin, jnp.zeros_like(sin), pad], axis=-1)
    two = lambda a: jnp.concatenate([a, a], axis=-1)
    return two(cos_h), two(sa_h), two(sb_h)


def _trunk(x, mem, w, s5m):
    nb, seq_len, _ = x.shape
    nc = seq_len // S5_ROW
    t = nb * seq_len
    cos_t, sa_t, sb_t = _rope_tables(seq_len)
    u, u_rows, q, k, v, xq = _inproj(x, w["norm_mix"], w["w_a"], cos_t, sa_t, sb_t)
    mk, mv = _memkv(mem.reshape(nb * N_MEM, D_MODEL), w["norm_mem"], w["w_mem_kv"])
    y_rows = _s5_core(u_rows.reshape(S5_GROUPS, nc * nb, S5_LANES), *s5m, nb=nb, nc=nc)
    flat = lambda a: a.reshape(t, 512)
    yd, yx = _attention(flat(q), flat(k), flat(v), flat(xq), mk, mv,
                        w["lq1"], w["lk1"], w["lq2"], w["lk2"], w["subln"], nb, seq_len)
    seq = lambda a: a.reshape(nb, seq_len, 512)
    x1 = _merge(x, u, y_rows.reshape(S5_GROUPS, nc, nb, S5_LANES), seq(yd), seq(yx),
                w["norm_mix"], w["w_g"], w["s5_d"], w["w_glu"], w["b_glu"], w["w_up"], w["w_out"])
    out = _ffn(x1.reshape(t, D_MODEL), w["norm_ffn"], w["w_ffn_in"], w["w_ffn_out"],
               w["norm_final"])
    return out.reshape(nb, seq_len, D_MODEL)


def kernel(x_prompt, x_sample, mem_prompt, mem_sample, norm_mix, norm_mem, w_in, s5_lambda_re, s5_lambda_im, s5_log_step, s5_b_re, s5_b_im, s5_c_re, s5_c_im, s5_d, s5_w_glu, s5_b_glu, diff_lambda_q1, diff_lambda_k1, diff_lambda_q2, diff_lambda_k2, diff_subln, w_mem_kv, w_up, w_out, norm_ffn, w_ffn_in, w_ffn_out, norm_final):
    l = 0
    w = dict(
        norm_mix=norm_mix[l][None, :], norm_mem=norm_mem[l][None, :],
        w_a=w_in[l][:, :A_COLS].astype(BF16), w_g=w_in[l][:, A_COLS:].astype(BF16),
        s5_d=s5_d[l][None, :], w_glu=s5_w_glu[l].astype(BF16), b_glu=s5_b_glu[l][None, :],
        lq1=diff_lambda_q1[l][None, :], lk1=diff_lambda_k1[l][None, :],
        lq2=diff_lambda_q2[l][None, :], lk2=diff_lambda_k2[l][None, :],
        subln=diff_subln[l][None, :], w_mem_kv=w_mem_kv[l].astype(BF16),
        w_up=w_up[l].astype(BF16), w_out=w_out[l].astype(BF16),
        norm_ffn=norm_ffn[l][None, :], w_ffn_in=w_ffn_in[l].astype(BF16),
        w_ffn_out=w_ffn_out[l].astype(BF16), norm_final=norm_final[None, :],
    )
    s5m = _s5_matrices(s5_lambda_re[l], s5_lambda_im[l], s5_log_step[l],
                       s5_b_re[l], s5_b_im[l], s5_c_re[l], s5_c_im[l])
    return (_trunk(x_prompt, mem_prompt, w, s5m), _trunk(x_sample, mem_sample, w, s5m))
```
